```python
import math
import jax, jax.numpy as jnp
from jax import lax
import numpy as np

D_MODEL = 1024
BATCH = 2
SEQ = 8192
DEPTH = 2
DEC_BATCH = 1
DEC_SEQ = 16384
PAST_LEN = 128

MIX_WIDTH = D_MODEL
FOURIER_WIDTH = MIX_WIDTH // 2
N_FGROUPS = 4
FGROUP = FOURIER_WIDTH // N_FGROUPS
ATT_WIDTH = MIX_WIDTH - FOURIER_WIDTH
HEAD_DIM = 64
N_HEADS = ATT_WIDTH // HEAD_DIM
IN_WIDTH = FOURIER_WIDTH + 3 * ATT_WIDTH
GRID_W = 64
WIN_ROWS = 8
WIN_COLS = 16
COL_BLOCK = 16
KEY_SPAN = COL_BLOCK + WIN_COLS
N_COL_BLOCKS = GRID_W // COL_BLOCK
D_FF = int(math.ceil(8 * D_MODEL / 3 / 256) * 256)
ALPHA = (2 * DEPTH) ** 0.25
BETA = (8 * DEPTH) ** -0.25
LN_EPS = 1e-5
NEG_INF = -1e30

kernel_name = "fnet_natten_hybrid_encoder"


def _layer_norm(x, g, b):
    xf = x.astype(jnp.float32)
    mu = jnp.mean(xf, axis=-1, keepdims=True)
    var = jnp.mean(jnp.square(xf - mu), axis=-1, keepdims=True)
    y = (xf - mu) * lax.rsqrt(var + LN_EPS)
    return (y * g.astype(jnp.float32) + b.astype(jnp.float32)).astype(x.dtype)


def _fourier_mix(u):
    B, S, _ = u.shape
    ug = u.reshape(B, S, N_FGROUPS, FGROUP).astype(jnp.float32)
    f = jnp.fft.fft2(ug, axes=(1, 3), norm="ortho").real
    return f.reshape(B, S, FOURIER_WIDTH).astype(u.dtype)


def _neighbourhood_attention(q, k, v, rpb):
    B, S, _ = q.shape
    rows = S // GRID_W
    kh = min(WIN_ROWS, rows)
    grid = (B, rows, GRID_W, N_HEADS, HEAD_DIM)
    q = q.reshape(grid)
    k = k.reshape(grid)
    v = v.reshape(grid)
    r = np.arange(rows)
    rs = np.clip(r - kh // 2, 0, rows - kh)
    row_idx = rs[:, None] + np.arange(kh)[None, :]
    dr_idx = row_idx - r[:, None] + (WIN_ROWS - 1)
    k_rows = jnp.take(k, row_idx, axis=1)
    v_rows = jnp.take(v, row_idx, axis=1)
    scale = HEAD_DIM ** -0.5
    outs = []
    for j in range(N_COL_BLOCKS):
        c0 = j * COL_BLOCK
        ks = int(np.clip(c0 - WIN_COLS // 2, 0, GRID_W - KEY_SPAN))
        cq = c0 + np.arange(COL_BLOCK)
        cs = np.clip(cq - WIN_COLS // 2, 0, GRID_W - WIN_COLS)
        kc = ks + np.arange(KEY_SPAN)
        mask = (kc[None, :] >= cs[:, None]) & (kc[None, :] < cs[:, None] + WIN_COLS)
        dc_idx = np.clip(kc[None, :] - cq[:, None] + (WIN_COLS - 1), 0, 2 * WIN_COLS - 2)
        bias = rpb[:, dr_idx[:, None, :, None], dc_idx[None, :, None, :]]
        bias = jnp.transpose(bias, (1, 0, 2, 3, 4)).astype(jnp.float32)
        qb = q[:, :, c0:c0 + COL_BLOCK]
        kb = k_rows[:, :, :, ks:ks + KEY_SPAN]
        vb = v_rows[:, :, :, ks:ks + KEY_SPAN]
        s = jnp.einsum('brqhd,brawhd->brhqaw', qb, kb).astype(jnp.float32) * scale + bias[None]
        s = jnp.where(jnp.asarray(mask)[:, None, :], s, NEG_INF)
        p = jax.nn.softmax(s.reshape(B, rows, N_HEADS, COL_BLOCK, kh * KEY_SPAN), axis=-1)
        p = p.reshape(s.shape).astype(v.dtype)
        outs.append(jnp.einsum('brhqaw,brawhd->brqhd', p, vb))
    out = jnp.concatenate(outs, axis=2)
    return out.reshape(B, S, ATT_WIDTH)


def _layer(x, w_in, w_out, rpb, ln1_g, ln1_b, w_gate, w_up, w_down, ln2_g, ln2_b):
    h = jnp.einsum('bsd,de->bse', x, w_in)
    u_f = h[..., :FOURIER_WIDTH]
    q = h[..., FOURIER_WIDTH:FOURIER_WIDTH + ATT_WIDTH]
    k = h[..., FOURIER_WIDTH + ATT_WIDTH:FOURIER_WIDTH + 2 * ATT_WIDTH]
    v = h[..., FOURIER_WIDTH + 2 * ATT_WIDTH:]
    y = jnp.concatenate([_fourier_mix(u_f), _neighbourhood_attention(q, k, v, rpb)], axis=-1)
    mix = jnp.einsum('bse,ed->bsd', y, w_out)
    x = _layer_norm(ALPHA * x + mix, ln1_g, ln1_b)
    hid = jax.nn.silu(jnp.einsum('bsd,df->bsf', x, w_gate)) * jnp.einsum('bsd,df->bsf', x, w_up)
    ffn = jnp.einsum('bsf,fd->bsd', hid, w_down)
    return _layer_norm(ALPHA * x + ffn, ln2_g, ln2_b)


def setup_inputs(seed: int = 0) -> dict:
    key = jax.random.key(seed)
    ks = jax.random.split(key, 14)
    f32 = jnp.float32
    x_prompt = jax.random.normal(ks[0], (BATCH, SEQ, D_MODEL), f32)
    x_sample = jax.random.normal(ks[1], (DEC_BATCH, DEC_SEQ, D_MODEL), f32)
    w_in = jax.random.normal(ks[2], (DEPTH, D_MODEL, IN_WIDTH), f32) * D_MODEL ** -0.5
    col_scale = jnp.concatenate([jnp.ones((IN_WIDTH - ATT_WIDTH,), f32), jnp.full((ATT_WIDTH,), BETA, f32)])
    w_in = w_in * col_scale
    w_out = jax.random.normal(ks[3], (DEPTH, MIX_WIDTH, D_MODEL), f32) * (MIX_WIDTH ** -0.5 * BETA)
    rpb = jax.random.normal(ks[4], (DEPTH, N_HEADS, 2 * WIN_ROWS - 1, 2 * WIN_COLS - 1), f32) * 0.1
    ln1_g = 1.0 + 0.01 * jax.random.normal(ks[5], (DEPTH, D_MODEL), f32)
    ln1_b = 0.01 * jax.random.normal(ks[6], (DEPTH, D_MODEL), f32)
    w_gate = jax.random.normal(ks[7], (DEPTH, D_MODEL, D_FF), f32) * D_MODEL ** -0.5
    w_up = jax.random.normal(ks[8], (DEPTH, D_MODEL, D_FF), f32) * (D_MODEL ** -0.5 * BETA)
    w_down = jax.random.normal(ks[9], (DEPTH, D_FF, D_MODEL), f32) * (D_FF ** -0.5 * BETA)
    ln2_g = 1.0 + 0.01 * jax.random.normal(ks[10], (DEPTH, D_MODEL), f32)
    ln2_b = 0.01 * jax.random.normal(ks[11], (DEPTH, D_MODEL), f32)
    return {"x_prompt": x_prompt, "x_sample": x_sample, "w_in": w_in, "w_out": w_out,
            "rpb": rpb, "ln1_g": ln1_g, "ln1_b": ln1_b, "w_gate": w_gate, "w_up": w_up,
            "w_down": w_down, "ln2_g": ln2_g, "ln2_b": ln2_b}


def reference(x_prompt, x_sample, w_in, w_out, rpb, ln1_g, ln1_b, w_gate, w_up, w_down, ln2_g, ln2_b):
    y_prompt = x_prompt
    y_sample = x_sample
    for l in range(DEPTH):
        params = (w_in[l], w_out[l], rpb[l], ln1_g[l], ln1_b[l], w_gate[l], w_up[l], w_down[l], ln2_g[l], ln2_b[l])
        y_prompt = _layer(y_prompt, *params)
        y_sample = _layer(y_sample, *params)
    return (y_prompt, y_sample)
```

```python
import functools
import math

import jax
import jax.numpy as jnp
import numpy as np
from jax import lax
from jax.experimental import pallas as pl
from jax.experimental.pallas import tpu as pltpu

F32 = jnp.float32
BF16 = jnp.bfloat16

D_MODEL = 1024
DEPTH = 2
FOURIER_WIDTH = 512
FGROUP = 128
N_FGROUPS = FOURIER_WIDTH // FGROUP
ATT_WIDTH = 512
HEAD_DIM = 64
N_HEADS = ATT_WIDTH // HEAD_DIM
IN_WIDTH = FOURIER_WIDTH + 3 * ATT_WIDTH
GRID_W = 64
WIN_ROWS = 8
WIN_COLS = 16
D_FF = 2816
ALPHA = (2 * DEPTH) ** 0.25
LN_EPS = 1e-5
NEG_INF = -1e30
ATT_SCALE = HEAD_DIM ** -0.5

DFT_N1 = 128
HEAD_PAIR = 2 * HEAD_DIM
KEY_ROWS = WIN_ROWS * GRID_W

V7X_VMEM_LIMIT = 56 * 1024 * 1024

TM_PROJ = 512
FF_CHUNK = 256
DFT1_COLS = 4096
DFT2_K1 = 8
OUT_SUB = 4
ATT_ROWS = 16


def _layer_norm(z, g, b):
    mu = jnp.mean(z, axis=-1, keepdims=True)
    zc = z - mu
    var = jnp.mean(zc * zc, axis=-1, keepdims=True)
    return zc * lax.rsqrt(var + LN_EPS) * g + b


@functools.lru_cache(maxsize=None)
def _dft_constants(seq):
    n1 = DFT_N1
    n2 = seq // n1
    kn = np.outer(np.arange(n1), np.arange(n1)) % n1
    th = 2.0 * np.pi * kn / n1
    f1 = np.stack([np.cos(th), -np.sin(th)]) / math.sqrt(n1)
    k = np.arange(n1)[:, None, None] + n1 * np.arange(n2)[None, :, None]
    m = (k * np.arange(n2)[None, None, :]) % seq
    th = 2.0 * np.pi * m / seq
    gr = np.cos(th) / math.sqrt(n2)
    gi = -np.sin(th) / math.sqrt(n2)
    g = np.concatenate([np.concatenate([gr, -gi], axis=2),
                        np.concatenate([gi, gr], axis=2)], axis=1)
    cn = np.outer(np.arange(FGROUP), np.arange(FGROUP)) % FGROUP
    th = 2.0 * np.pi * cn / FGROUP
    cs = np.stack([np.cos(th), np.sin(th)]) / math.sqrt(FGROUP)
    return f1.astype(np.float32), g.astype(np.float32), cs.astype(np.float32)


def _bias_tables(rpb_l):
    a = np.arange(WIN_ROWS)
    d = np.arange(WIN_ROWS)
    c = np.arange(GRID_W)
    w = np.arange(GRID_W)
    dr = a[None, :] - d[:, None] + (WIN_ROWS - 1)
    dc = np.clip(w[None, :] - c[:, None] + (WIN_COLS - 1), 0, 2 * WIN_COLS - 2)
    cs = np.clip(c - WIN_COLS // 2, 0, GRID_W - WIN_COLS)
    mask = (w[None, :] >= cs[:, None]) & (w[None, :] < cs[:, None] + WIN_COLS)
    tbl = rpb_l[:, dr[:, None, :, None], dc[None, :, None, :]]
    tbl = jnp.where(jnp.asarray(mask)[None, None, :, None, :], tbl.astype(F32), NEG_INF)
    tbl = tbl.reshape(N_HEADS // 2, 2, WIN_ROWS, GRID_W, KEY_ROWS)
    tbl = jnp.transpose(tbl, (0, 2, 1, 3, 4))
    return tbl.reshape(N_HEADS // 2, WIN_ROWS, HEAD_PAIR, KEY_ROWS)


def _in_proj_kernel(x_ref, w_ref, u_ref, q_ref, k_ref, v_ref):
    xb = x_ref[...].astype(BF16)
    lo = 0
    for o_ref, scale in ((u_ref, None), (q_ref, ATT_SCALE), (k_ref, None), (v_ref, None)):
        width = o_ref.shape[-1]
        h = jnp.dot(xb, w_ref[:, lo:lo + width], preferred_element_type=F32)
        if scale is not None:
            h = h * scale
        o_ref[...] = h.astype(BF16)
        lo += width


def _in_proj(x2, w_in_b):
    t = x2.shape[0]
    tm = TM_PROJ
    widths = (FOURIER_WIDTH, ATT_WIDTH, ATT_WIDTH, ATT_WIDTH)
    return pl.pallas_call(
        _in_proj_kernel,
        grid=(t // tm,),
        in_specs=[pl.BlockSpec((tm, D_MODEL), lambda i: (i, 0)),
                  pl.BlockSpec((D_MODEL, IN_WIDTH), lambda i: (0, 0))],
        out_specs=[pl.BlockSpec((tm, wd), lambda i: (i, 0)) for wd in widths],
        out_shape=[jax.ShapeDtypeStruct((t, wd), BF16) for wd in widths],
        compiler_params=pltpu.CompilerParams(
            dimension_semantics=("parallel",), vmem_limit_bytes=V7X_VMEM_LIMIT),
        name="in_proj",
    )(x2, w_in_b)


def _dft1_kernel(f_ref, x_ref, y_ref):
    y_ref[0] = jnp.dot(f_ref[0], x_ref[0], preferred_element_type=F32).astype(BF16)


def _dft_stage1(u3, f1_b):
    b, n1, cols = u3.shape
    ct = min(DFT1_COLS, cols)
    ncol = cols // ct
    return pl.pallas_call(
        _dft1_kernel,
        grid=(b, 2, ncol),
        in_specs=[pl.BlockSpec((1, n1, n1), lambda bi, c, j: (c, 0, 0)),
                  pl.BlockSpec((1, n1, ct), lambda bi, c, j: (bi, 0, j))],
        out_specs=pl.BlockSpec((1, n1, ct), lambda bi, c, j: (bi, 0, c * ncol + j)),
        out_shape=jax.ShapeDtypeStruct((b, n1, 2 * cols), BF16),
        compiler_params=pltpu.CompilerParams(
            dimension_semantics=("parallel", "parallel", "parallel"),
            vmem_limit_bytes=V7X_VMEM_LIMIT),
        name="dft_stage1",
    )(f1_b, u3)


def _dft2_kernel(g_ref, y_ref, x_ref):
    for kl in range(g_ref.shape[0]):
        x_ref[0, kl] = jnp.dot(g_ref[kl], y_ref[0, kl], preferred_element_type=F32).astype(BF16)


def _dft_stage2(y4, g_b):
    b, n1, n2x2, ch = y4.shape
    kt = DFT2_K1
    return pl.pallas_call(
        _dft2_kernel,
        grid=(n1 // kt, b),
        in_specs=[pl.BlockSpec((kt, n2x2, n2x2), lambda t, bi: (t, 0, 0)),
                  pl.BlockSpec((1, kt, n2x2, ch), lambda t, bi: (bi, t, 0, 0))],
        out_specs=pl.BlockSpec((1, kt, n2x2, ch), lambda t, bi: (bi, t, 0, 0)),
        out_shape=jax.ShapeDtypeStruct((b, n1, n2x2, ch), BF16),
        compiler_params=pltpu.CompilerParams(
            dimension_semantics=("parallel", "parallel"), vmem_limit_bytes=V7X_VMEM_LIMIT),
        name="dft_stage2",
    )(g_b, y4)


def _attn_kernel(q_ref, k_ref, v_ref, bias_ref, o_ref, *, rows, rq):
    rc = pl.program_id(2)
    lane = lax.broadcasted_iota(jnp.int32, (GRID_W, HEAD_PAIR), 1)
    first_head = lane < HEAD_DIM

    def body(i, carry):
        r = rc * rq + i
        rs = jnp.clip(r - WIN_ROWS // 2, 0, rows - WIN_ROWS)
        d = r - rs
        q0 = pl.multiple_of(i * GRID_W, GRID_W)
        k0 = pl.multiple_of(rs * GRID_W, GRID_W)
        q2 = q_ref[0, pl.ds(q0, GRID_W), :]
        zero = jnp.zeros_like(q2)
        qs = jnp.concatenate([jnp.where(first_head, q2, zero),
                              jnp.where(first_head, zero, q2)], axis=0)
        kk = k_ref[0, pl.ds(k0, KEY_ROWS), :]
        vv = v_ref[0, pl.ds(k0, KEY_ROWS), :]
        s = lax.dot_general(qs, kk, (((1,), (1,)), ((), ())), preferred_element_type=F32)
        s = s + bias_ref[0, d]
        m = jnp.max(s, axis=-1, keepdims=True)
        p = jnp.exp(s - m)
        l = jnp.sum(p, axis=-1, keepdims=True)
        o = jnp.dot(p.astype(BF16), vv, preferred_element_type=F32)
        o = o / l
        o2 = jnp.where(first_head, o[:GRID_W], o[GRID_W:])
        o_ref[0, pl.ds(q0, GRID_W), :] = o2.astype(BF16)
        return carry

    lax.fori_loop(0, rq, body, 0)


def _attention(q3, k3, v3, bias):
    b, s, _ = q3.shape
    rows = s // GRID_W
    rq = ATT_ROWS
    nhp = N_HEADS // 2
    kern = functools.partial(_attn_kernel, rows=rows, rq=rq)
    return pl.pallas_call(
        kern,
        grid=(b, nhp, rows // rq),
        in_specs=[pl.BlockSpec((1, rq * GRID_W, HEAD_PAIR), lambda bi, hp, rc: (bi, rc, hp)),
                  pl.BlockSpec((1, s, HEAD_PAIR), lambda bi, hp, rc: (bi, 0, hp)),
                  pl.BlockSpec((1, s, HEAD_PAIR), lambda bi, hp, rc: (bi, 0, hp)),
                  pl.BlockSpec((1, WIN_ROWS, HEAD_PAIR, KEY_ROWS), lambda bi, hp, rc: (hp, 0, 0, 0))],
        out_specs=pl.BlockSpec((1, rq * GRID_W, HEAD_PAIR), lambda bi, hp, rc: (bi, rc, hp)),
        out_shape=jax.ShapeDtypeStruct((b, s, ATT_WIDTH), BF16),
        compiler_params=pltpu.CompilerParams(
            dimension_semantics=("parallel", "parallel", "arbitrary"),
            vmem_limit_bytes=V7X_VMEM_LIMIT),
        name="nbr_attention",
    )(q3, k3, v3, bias)


def _out_proj_kernel(xre_ref, xim_ref, a_ref, x_ref, cs_ref, wf_ref, wa_ref, g_ref, b_ref,
                     o_ref, of_ref):
    n1 = xre_ref.shape[1]
    for j in range(OUT_SUB):
        for g in range(N_FGROUPS):
            lo = j * FOURIER_WIDTH + g * FGROUP
            f = jnp.dot(xre_ref[0, :, lo:lo + FGROUP], cs_ref[0], preferred_element_type=F32)
            f = f + jnp.dot(xim_ref[0, :, lo:lo + FGROUP], cs_ref[1], preferred_element_type=F32)
            of_ref[j * n1:(j + 1) * n1, g * FGROUP:(g + 1) * FGROUP] = f.astype(BF16)
    mix = jnp.dot(of_ref[...], wf_ref[...], preferred_element_type=F32)
    mix = mix + jnp.dot(a_ref[0], wa_ref[...], preferred_element_type=F32)
    z = ALPHA * x_ref[0] + mix
    o_ref[0] = _layer_norm(z, g_ref[...], b_ref[...])


def _out_proj(xf, a3, x3, cs_b, w_out_b, g, bta):
    b, s, _ = x3.shape
    n1 = DFT_N1
    n2 = s // n1
    m = OUT_SUB
    tm = m * n1
    nt = n2 // m
    fw = m * FOURIER_WIDTH
    const = lambda bi, t: (0, 0)
    return pl.pallas_call(
        _out_proj_kernel,
        grid=(b, nt),
        in_specs=[pl.BlockSpec((1, n1, fw), lambda bi, t: (bi, 0, t)),
                  pl.BlockSpec((1, n1, fw), lambda bi, t: (bi, 0, nt + t)),
                  pl.BlockSpec((1, tm, ATT_WIDTH), lambda bi, t: (bi, t, 0)),
                  pl.BlockSpec((1, tm, D_MODEL), lambda bi, t: (bi, t, 0)),
                  pl.BlockSpec((2, FGROUP, FGROUP), lambda bi, t: (0, 0, 0)),
                  pl.BlockSpec((FOURIER_WIDTH, D_MODEL), lambda bi, t: (0, 0)),
                  pl.BlockSpec((ATT_WIDTH, D_MODEL), lambda bi, t: (1, 0)),
                  pl.BlockSpec((1, D_MODEL), const),
                  pl.BlockSpec((1, D_MODEL), const)],
        out_specs=pl.BlockSpec((1, tm, D_MODEL), lambda bi, t: (bi, t, 0)),
        out_shape=jax.ShapeDtypeStruct((b, s, D_MODEL), F32),
        scratch_shapes=[pltpu.VMEM((tm, FOURIER_WIDTH), BF16)],
        compiler_params=pltpu.CompilerParams(
            dimension_semantics=("parallel", "parallel"), vmem_limit_bytes=V7X_VMEM_LIMIT),
        name="out_proj_ln",
    )(xf, xf, a3, x3, cs_b, w_out_b, w_out_b, g, bta)


def _ffn_kernel(x_ref, wg_ref, wu_ref, wd_ref, g_ref, b_ref, o_ref, acc_ref):
    x = x_ref[...]
    xb = x.astype(BF16)
    for c in range(D_FF // FF_CHUNK):
        sl = slice(c * FF_CHUNK, (c + 1) * FF_CHUNK)
        gate = jnp.dot(xb, wg_ref[:, sl], preferred_element_type=F32)
        up = jnp.dot(xb, wu_ref[:, sl], preferred_element_type=F32)
        hid = (gate * jax.nn.sigmoid(gate) * up).astype(BF16)
        part = jnp.dot(hid, wd_ref[sl, :], preferred_element_type=F32)
        if c == 0:
            acc_ref[...] = part
        else:
            acc_ref[...] += part
    z = ALPHA * x + acc_ref[...]
    o_ref[...] = _layer_norm(z, g_ref[...], b_ref[...])


def _ffn(x2, wg_b, wu_b, wd_b, g, bta):
    t = x2.shape[0]
    tm = TM_PROJ
    const = lambda i: (0, 0)
    return pl.pallas_call(
        _ffn_kernel,
        grid=(t // tm,),
        in_specs=[pl.BlockSpec((tm, D_MODEL), lambda i: (i, 0)),
                  pl.BlockSpec((D_MODEL, D_FF), const),
                  pl.BlockSpec((D_MODEL, D_FF), const),
                  pl.BlockSpec((D_FF, D_MODEL), const),
                  pl.BlockSpec((1, D_MODEL), const),
                  pl.BlockSpec((1, D_MODEL), const)],
        out_specs=pl.BlockSpec((tm, D_MODEL), lambda i: (i, 0)),
        out_shape=jax.ShapeDtypeStruct((t, D_MODEL), F32),
        scratch_shapes=[pltpu.VMEM((tm, D_MODEL), F32)],
        compiler_params=pltpu.CompilerParams(
            dimension_semantics=("parallel",), vmem_limit_bytes=V7X_VMEM_LIMIT),
        name="ffn_ln",
    )(x2, wg_b, wu_b, wd_b, g, bta)


def _layer(x3, p):
    b, s, d = x3.shape
    n1 = DFT_N1
    n2 = s // n1
    f1, g, cs = _dft_constants(s)
    f1_b = jnp.asarray(f1).astype(BF16)
    g_b = jnp.asarray(g).astype(BF16)
    cs_b = jnp.asarray(cs).astype(BF16)

    u, q, k, v = _in_proj(x3.reshape(b * s, d), p["w_in"])
    y = _dft_stage1(u.reshape(b, n1, n2 * FOURIER_WIDTH), f1_b)
    xf = _dft_stage2(y.reshape(b, n1, 2 * n2, FOURIER_WIDTH), g_b)
    xf = xf.reshape(b, n1, 2 * n2 * FOURIER_WIDTH)
    att = _attention(q.reshape(b, s, ATT_WIDTH), k.reshape(b, s, ATT_WIDTH),
                     v.reshape(b, s, ATT_WIDTH), p["bias"])
    x1 = _out_proj(xf, att, x3, cs_b, p["w_out"], p["ln1_g"], p["ln1_b"])
    x2 = _ffn(x1.reshape(b * s, d), p["w_gate"], p["w_up"], p["w_down"], p["ln2_g"], p["ln2_b"])
    return x2.reshape(b, s, d)


def kernel(x_prompt, x_sample, w_in, w_out, rpb, ln1_g, ln1_b, w_gate, w_up, w_down, ln2_g, ln2_b):
    y_prompt = x_prompt
    y_sample = x_sample
    for l in range(DEPTH):
        p = dict(
            w_in=w_in[l].astype(BF16), w_out=w_out[l].astype(BF16),
            w_gate=w_gate[l].astype(BF16), w_up=w_up[l].astype(BF16), w_down=w_down[l].astype(BF16),
            bias=_bias_tables(rpb[l]),
            ln1_g=ln1_g[l].reshape(1, D_MODEL), ln1_b=ln1_b[l].reshape(1, D_MODEL),
            ln2_g=ln2_g[l].reshape(1, D_MODEL), ln2_b=ln2_b[l].reshape(1, D_MODEL),
        )
        y_prompt = _layer(y_prompt, p)
        y_sample = _layer(y_sample, p)
    return (y_prompt, y_sample)
```

```python
import functools
import math

import jax
import jax.numpy as jnp
import numpy as np
from jax import lax
from jax.experimental import pallas as pl
from jax.experimental.pallas import tpu as pltpu

F32 = jnp.float32
BF16 = jnp.bfloat16

D_MODEL = 1024
DEPTH = 2
FOURIER_WIDTH = 512
FGROUP = 128
N_FGROUPS = FOURIER_WIDTH // FGROUP
ATT_WIDTH = 512
HEAD_DIM = 64
N_HEADS = ATT_WIDTH // HEAD_DIM
IN_WIDTH = FOURIER_WIDTH + 3 * ATT_WIDTH
GRID_W = 64
WIN_ROWS = 8
WIN_COLS = 16
D_FF = 2816
ALPHA = (2 * DEPTH) ** 0.25
LN_EPS = 1e-5
NEG_INF = -1e30
ATT_SCALE = HEAD_DIM ** -0.5

DFT_N1 = 128
HEAD_PAIR = 2 * HEAD_DIM
KEY_ROWS = WIN_ROWS * GRID_W

V7X_VMEM_LIMIT = 56 * 1024 * 1024

TM_PROJ = 512
FF_CHUNK = 256
DFT1_COLS = 4096
DFT2_K1 = 8
OUT_SUB = 4
ATT_ROWS = 16
ATT_GROUP = 4


def _layer_norm(z, g, b):
    mu = jnp.mean(z, axis=-1, keepdims=True)
    zc = z - mu
    var = jnp.mean(zc * zc, axis=-1, keepdims=True)
    return zc * lax.rsqrt(var + LN_EPS) * g + b


@functools.lru_cache(maxsize=None)
def _dft_constants(seq):
    n1 = DFT_N1
    n2 = seq // n1
    kn = np.outer(np.arange(n1), np.arange(n1)) % n1
    th = 2.0 * np.pi * kn / n1
    f1 = np.stack([np.cos(th), -np.sin(th)]) / math.sqrt(n1)
    k = np.arange(n1)[:, None, None] + n1 * np.arange(n2)[None, :, None]
    m = (k * np.arange(n2)[None, None, :]) % seq
    th = 2.0 * np.pi * m / seq
    gr = np.cos(th) / math.sqrt(n2)
    gi = -np.sin(th) / math.sqrt(n2)
    g = np.concatenate([np.concatenate([gr, -gi], axis=2),
                        np.concatenate([gi, gr], axis=2)], axis=1)
    cn = np.outer(np.arange(FGROUP), np.arange(FGROUP)) % FGROUP
    th = 2.0 * np.pi * cn / FGROUP
    cs = np.stack([np.cos(th), np.sin(th)]) / math.sqrt(FGROUP)
    return f1.astype(np.float32), g.astype(np.float32), cs.astype(np.float32)


N_DR = 2 * WIN_ROWS - 1
N_DC = 2 * WIN_COLS - 1


def _bias_kernel(rpb_ref, o_ref):
    h = pl.program_id(0)
    c = lax.broadcasted_iota(jnp.int32, (GRID_W, HEAD_PAIR), 0)
    w = lax.broadcasted_iota(jnp.int32, (GRID_W, HEAD_PAIR), 1) % GRID_W
    diff = w - c + (WIN_COLS - 1)
    cs = jnp.clip(c - WIN_COLS // 2, 0, GRID_W - WIN_COLS)
    valid = (w >= cs) & (w < cs + WIN_COLS)
    for dr in range(N_DR):
        base = (h * N_DR + dr) * N_DC
        t = jnp.full((GRID_W, HEAD_PAIR), NEG_INF, F32)
        for dc in range(N_DC):
            t = jnp.where(diff == dc, rpb_ref[base + dc], t)
        t = jnp.where(valid, t, NEG_INF)
        for d in range(WIN_ROWS):
            a = dr - (WIN_ROWS - 1) + d
            if 0 <= a < WIN_ROWS:
                half = (a % 2) * GRID_W
                o_ref[0, d, :, a * GRID_W:(a + 1) * GRID_W] = t[:, half:half + GRID_W]


def _bias_tables(rpb_l):
    return pl.pallas_call(
        _bias_kernel,
        grid=(N_HEADS,),
        in_specs=[pl.BlockSpec(memory_space=pltpu.SMEM)],
        out_specs=pl.BlockSpec((1, WIN_ROWS, GRID_W, KEY_ROWS), lambda h: (h // 2, 0, h % 2, 0)),
        out_shape=jax.ShapeDtypeStruct((N_HEADS // 2, WIN_ROWS, HEAD_PAIR, KEY_ROWS), F32),
        compiler_params=pltpu.CompilerParams(dimension_semantics=("parallel",)),
        name="bias_expand",
    )(rpb_l.astype(F32).reshape(-1))


def _in_proj_kernel(x_ref, w_ref, u_ref, q_ref, k_ref, v_ref):
    xb = x_ref[...].astype(BF16)
    lo = 0
    for o_ref, scale in ((u_ref, None), (q_ref, ATT_SCALE), (k_ref, None), (v_ref, None)):
        width = o_ref.shape[-1]
        h = jnp.dot(xb, w_ref[:, lo:lo + width], preferred_element_type=F32)
        if scale is not None:
            h = h * scale
        o_ref[...] = h.astype(BF16)
        lo += width


def _in_proj(x2, w_in_b):
    t = x2.shape[0]
    tm = TM_PROJ
    widths = (FOURIER_WIDTH, ATT_WIDTH, ATT_WIDTH, ATT_WIDTH)
    return pl.pallas_call(
        _in_proj_kernel,
        grid=(t // tm,),
        in_specs=[pl.BlockSpec((tm, D_MODEL), lambda i: (i, 0)),
                  pl.BlockSpec((D_MODEL, IN_WIDTH), lambda i: (0, 0))],
        out_specs=[pl.BlockSpec((tm, wd), lambda i: (i, 0)) for wd in widths],
        out_shape=[jax.ShapeDtypeStruct((t, wd), BF16) for wd in widths],
        compiler_params=pltpu.CompilerParams(
            dimension_semantics=("parallel",), vmem_limit_bytes=V7X_VMEM_LIMIT),
        name="in_proj",
    )(x2, w_in_b)


def _dft1_kernel(f_ref, x_ref, y_ref):
    y_ref[0] = jnp.dot(f_ref[0], x_ref[0], preferred_element_type=F32).astype(BF16)


def _dft_stage1(u3, f1_b):
    b, n1, cols = u3.shape
    ct = min(DFT1_COLS, cols)
    ncol = cols // ct
    return pl.pallas_call(
        _dft1_kernel,
        grid=(b, 2, ncol),
        in_specs=[pl.BlockSpec((1, n1, n1), lambda bi, c, j: (c, 0, 0)),
                  pl.BlockSpec((1, n1, ct), lambda bi, c, j: (bi, 0, j))],
        out_specs=pl.BlockSpec((1, n1, ct), lambda bi, c, j: (bi, 0, c * ncol + j)),
        out_shape=jax.ShapeDtypeStruct((b, n1, 2 * cols), BF16),
        compiler_params=pltpu.CompilerParams(
            dimension_semantics=("parallel", "parallel", "parallel"),
            vmem_limit_bytes=V7X_VMEM_LIMIT),
        name="dft_stage1",
    )(f1_b, u3)


def _dft2_kernel(g_ref, y_ref, x_ref):
    for kl in range(g_ref.shape[0]):
        x_ref[0, kl] = jnp.dot(g_ref[kl], y_ref[0, kl], preferred_element_type=F32).astype(BF16)


def _dft_stage2(y4, g_b):
    b, n1, n2x2, ch = y4.shape
    kt = DFT2_K1
    return pl.pallas_call(
        _dft2_kernel,
        grid=(n1 // kt, b),
        in_specs=[pl.BlockSpec((kt, n2x2, n2x2), lambda t, bi: (t, 0, 0)),
                  pl.BlockSpec((1, kt, n2x2, ch), lambda t, bi: (bi, t, 0, 0))],
        out_specs=pl.BlockSpec((1, kt, n2x2, ch), lambda t, bi: (bi, t, 0, 0)),
        out_shape=jax.ShapeDtypeStruct((b, n1, n2x2, ch), BF16),
        compiler_params=pltpu.CompilerParams(
            dimension_semantics=("parallel", "parallel"), vmem_limit_bytes=V7X_VMEM_LIMIT),
        name="dft_stage2",
    )(g_b, y4)


def _attn_kernel(q_ref, k_ref, v_ref, bias_ref, o_ref, *, rows, rq):
    rc = pl.program_id(2)
    lane = lax.broadcasted_iota(jnp.int32, (GRID_W, HEAD_PAIR), 1)
    first_head = lane < HEAD_DIM

    def body(g, carry):
        q0s, k0s, ss = [], [], []
        for j in range(ATT_GROUP):
            i = g * ATT_GROUP + j
            r = rc * rq + i
            rs = jnp.clip(r - WIN_ROWS // 2, 0, rows - WIN_ROWS)
            q0 = pl.multiple_of(i * GRID_W, GRID_W)
            k0 = pl.multiple_of(rs * GRID_W, GRID_W)
            q2 = q_ref[0, pl.ds(q0, GRID_W), :]
            zero = jnp.zeros_like(q2)
            qs = jnp.concatenate([jnp.where(first_head, q2, zero),
                                  jnp.where(first_head, zero, q2)], axis=0)
            kk = k_ref[0, pl.ds(k0, KEY_ROWS), :]
            s = lax.dot_general(qs, kk, (((1,), (1,)), ((), ())), preferred_element_type=F32)
            ss.append(s + bias_ref[0, r - rs])
            q0s.append(q0)
            k0s.append(k0)
        ps, ls = [], []
        for s in ss:
            m = jnp.max(s, axis=-1, keepdims=True)
            p = jnp.exp(s - m)
            ls.append(jnp.sum(p, axis=-1, keepdims=True))
            ps.append(p.astype(BF16))
        for q0, k0, p, l in zip(q0s, k0s, ps, ls):
            vv = v_ref[0, pl.ds(k0, KEY_ROWS), :]
            o = jnp.dot(p, vv, preferred_element_type=F32) / l
            o2 = jnp.where(first_head, o[:GRID_W], o[GRID_W:])
            o_ref[0, pl.ds(q0, GRID_W), :] = o2.astype(BF16)
        return carry

    lax.fori_loop(0, rq // ATT_GROUP, body, 0)


def _attention(q3, k3, v3, bias):
    b, s, _ = q3.shape
    rows = s // GRID_W
    rq = ATT_ROWS
    nhp = N_HEADS // 2
    kern = functools.partial(_attn_kernel, rows=rows, rq=rq)
    return pl.pallas_call(
        kern,
        grid=(b, nhp, rows // rq),
        in_specs=[pl.BlockSpec((1, rq * GRID_W, HEAD_PAIR), lambda bi, hp, rc: (bi, rc, hp)),
                  pl.BlockSpec((1, s, HEAD_PAIR), lambda bi, hp, rc: (bi, 0, hp)),
                  pl.BlockSpec((1, s, HEAD_PAIR), lambda bi, hp, rc: (bi, 0, hp)),
                  pl.BlockSpec((1, WIN_ROWS, HEAD_PAIR, KEY_ROWS), lambda bi, hp, rc: (hp, 0, 0, 0))],
        out_specs=pl.BlockSpec((1, rq * GRID_W, HEAD_PAIR), lambda bi, hp, rc: (bi, rc, hp)),
        out_shape=jax.ShapeDtypeStruct((b, s, ATT_WIDTH), BF16),
        compiler_params=pltpu.CompilerParams(
            dimension_semantics=("parallel", "parallel", "arbitrary"),
            vmem_limit_bytes=V7X_VMEM_LIMIT),
        name="nbr_attention",
    )(q3, k3, v3, bias)


def _out_proj_kernel(xre_ref, xim_ref, a_ref, x_ref, cs_ref, wf_ref, wa_ref, g_ref, b_ref,
                     o_ref, of_ref):
    n1 = xre_ref.shape[1]
    for j in range(OUT_SUB):
        for g in range(N_FGROUPS):
            lo = j * FOURIER_WIDTH + g * FGROUP
            f = jnp.dot(xre_ref[0, :, lo:lo + FGROUP], cs_ref[0], preferred_element_type=F32)
            f = f + jnp.dot(xim_ref[0, :, lo:lo + FGROUP], cs_ref[1], preferred_element_type=F32)
            of_ref[j * n1:(j + 1) * n1, g * FGROUP:(g + 1) * FGROUP] = f.astype(BF16)
    mix = jnp.dot(of_ref[...], wf_ref[...], preferred_element_type=F32)
    mix = mix + jnp.dot(a_ref[0], wa_ref[...], preferred_element_type=F32)
    z = ALPHA * x_ref[0] + mix
    o_ref[0] = _layer_norm(z, g_ref[...], b_ref[...])


def _out_proj(xf, a3, x3, cs_b, w_out_b, g, bta):
    b, s, _ = x3.shape
    n1 = DFT_N1
    n2 = s // n1
    m = OUT_SUB
    tm = m * n1
    nt = n2 // m
    fw = m * FOURIER_WIDTH
    const = lambda bi, t: (0, 0)
    return pl.pallas_call(
        _out_proj_kernel,
        grid=(b, nt),
        in_specs=[pl.BlockSpec((1, n1, fw), lambda bi, t: (bi, 0, t)),
                  pl.BlockSpec((1, n1, fw), lambda bi, t: (bi, 0, nt + t)),
                  pl.BlockSpec((1, tm, ATT_WIDTH), lambda bi, t: (bi, t, 0)),
                  pl.BlockSpec((1, tm, D_MODEL), lambda bi, t: (bi, t, 0)),
                  pl.BlockSpec((2, FGROUP, FGROUP), lambda bi, t: (0, 0, 0)),
                  pl.BlockSpec((FOURIER_WIDTH, D_MODEL), lambda bi, t: (0, 0)),
                  pl.BlockSpec((ATT_WIDTH, D_MODEL), lambda bi, t: (1, 0)),
                  pl.BlockSpec((1, D_MODEL), const),
                  pl.BlockSpec((1, D_MODEL), const)],
        out_specs=pl.BlockSpec((1, tm, D_MODEL), lambda bi, t: (bi, t, 0)),
        out_shape=jax.ShapeDtypeStruct((b, s, D_MODEL), F32),
        scratch_shapes=[pltpu.VMEM((tm, FOURIER_WIDTH), BF16)],
        compiler_params=pltpu.CompilerParams(
            dimension_semantics=("parallel", "parallel"), vmem_limit_bytes=V7X_VMEM_LIMIT),
        name="out_proj_ln",
    )(xf, xf, a3, x3, cs_b, w_out_b, w_out_b, g, bta)


def _ffn_kernel(x_ref, wg_ref, wu_ref, wd_ref, g_ref, b_ref, o_ref, acc_ref):
    x = x_ref[...]
    xb = x.astype(BF16)
    for c in range(D_FF // FF_CHUNK):
        sl = slice(c * FF_CHUNK, (c + 1) * FF_CHUNK)
        gate = jnp.dot(xb, wg_ref[:, sl], preferred_element_type=F32)
        up = jnp.dot(xb, wu_ref[:, sl], preferred_element_type=F32)
        hid = (gate * jax.nn.sigmoid(gate) * up).astype(BF16)
        part = jnp.dot(hid, wd_ref[sl, :], preferred_element_type=F32)
        if c == 0:
            acc_ref[...] = part
        else:
            acc_ref[...] += part
    z = ALPHA * x + acc_ref[...]
    o_ref[...] = _layer_norm(z, g_ref[...], b_ref[...])


def _ffn(x2, wg_b, wu_b, wd_b, g, bta):
    t = x2.shape[0]
    tm = TM_PROJ
    const = lambda i: (0, 0)
    return pl.pallas_call(
        _ffn_kernel,
        grid=(t // tm,),
        in_specs=[pl.BlockSpec((tm, D_MODEL), lambda i: (i, 0)),
                  pl.BlockSpec((D_MODEL, D_FF), const),
                  pl.BlockSpec((D_MODEL, D_FF), const),
                  pl.BlockSpec((D_FF, D_MODEL), const),
                  pl.BlockSpec((1, D_MODEL), const),
                  pl.BlockSpec((1, D_MODEL), const)],
        out_specs=pl.BlockSpec((tm, D_MODEL), lambda i: (i, 0)),
        out_shape=jax.ShapeDtypeStruct((t, D_MODEL), F32),
        scratch_shapes=[pltpu.VMEM((tm, D_MODEL), F32)],
        compiler_params=pltpu.CompilerParams(
            dimension_semantics=("parallel",), vmem_limit_bytes=V7X_VMEM_LIMIT),
        name="ffn_ln",
    )(x2, wg_b, wu_b, wd_b, g, bta)


def _layer(x3, p):
    b, s, d = x3.shape
    n1 = DFT_N1
    n2 = s // n1
    f1, g, cs = _dft_constants(s)
    f1_b = jnp.asarray(f1).astype(BF16)
    g_b = jnp.asarray(g).astype(BF16)
    cs_b = jnp.asarray(cs).astype(BF16)

    u, q, k, v = _in_proj(x3.reshape(b * s, d), p["w_in"])
    y = _dft_stage1(u.reshape(b, n1, n2 * FOURIER_WIDTH), f1_b)
    xf = _dft_stage2(y.reshape(b, n1, 2 * n2, FOURIER_WIDTH), g_b)
    xf = xf.reshape(b, n1, 2 * n2 * FOURIER_WIDTH)
    att = _attention(q.reshape(b, s, ATT_WIDTH), k.reshape(b, s, ATT_WIDTH),
                     v.reshape(b, s, ATT_WIDTH), p["bias"])
    x1 = _out_proj(xf, att, x3, cs_b, p["w_out"], p["ln1_g"], p["ln1_b"])
    x2 = _ffn(x1.reshape(b * s, d), p["w_gate"], p["w_up"], p["w_down"], p["ln2_g"], p["ln2_b"])
    return x2.reshape(b, s, d)


def kernel(x_prompt, x_sample, w_in, w_out, rpb, ln1_g, ln1_b, w_gate, w_up, w_down, ln2_g, ln2_b):
    y_prompt = x_prompt
    y_sample = x_sample
    for l in range(DEPTH):
        p = dict(
            w_in=w_in[l].astype(BF16), w_out=w_out[l].astype(BF16),
            w_gate=w_gate[l].astype(BF16), w_up=w_up[l].astype(BF16), w_down=w_down[l].astype(BF16),
            bias=_bias_tables(rpb[l]),
            ln1_g=ln1_g[l].reshape(1, D_MODEL), ln1_b=ln1_b[l].reshape(1, D_MODEL),
            ln2_g=ln2_g[l].reshape(1, D_MODEL), ln2_b=ln2_b[l].reshape(1, D_MODEL),
        )
        y_prompt = _layer(y_prompt, p)
        y_sample = _layer(y_sample, p)
    return (y_prompt, y_sample)
```

```python
import functools
import math

import jax
import jax.numpy as jnp
import numpy as np
from jax import lax
from jax.experimental import pallas as pl
from jax.experimental.pallas import tpu as pltpu

F32 = jnp.float32
BF16 = jnp.bfloat16

D_MODEL = 1024
DEPTH = 2
FOURIER_WIDTH = 512
FGROUP = 128
N_FGROUPS = FOURIER_WIDTH // FGROUP
ATT_WIDTH = 512
HEAD_DIM = 64
N_HEADS = ATT_WIDTH // HEAD_DIM
IN_WIDTH = FOURIER_WIDTH + 3 * ATT_WIDTH
GRID_W = 64
WIN_ROWS = 8
WIN_COLS = 16
D_FF = 2816
ALPHA = (2 * DEPTH) ** 0.25
LN_EPS = 1e-5
NEG_INF = -1e30
ATT_SCALE = HEAD_DIM ** -0.5

DFT_N1 = 128
HEAD_PAIR = 2 * HEAD_DIM
KEY_ROWS = WIN_ROWS * GRID_W

V7X_VMEM_LIMIT = 56 * 1024 * 1024

TM_PROJ = 512
FF_CHUNK = 256
FOURIER_ROW_PAD = 8
FOURIER_UNROLL = 16
ATT_ROWS = 16
ATT_GROUP = 1
ATT_AHEAD = 3


def _layer_norm(z, g, b):
    mu = jnp.mean(z, axis=-1, keepdims=True)
    zc = z - mu
    var = jnp.mean(zc * zc, axis=-1, keepdims=True)
    return zc * lax.rsqrt(var + LN_EPS) * g + b


@functools.lru_cache(maxsize=None)
def _dft_constants(seq):
    n1 = DFT_N1
    n2 = seq // n1
    n = np.arange(n1)[None, None, :] * n2 + np.arange(n2)[:, None, None]
    m = (np.arange(n1)[None, :, None] * n) % seq
    th = 2.0 * np.pi * m / seq
    f1 = np.concatenate([np.cos(th), -np.sin(th)], axis=1) / math.sqrt(n1)
    kn = np.outer(np.arange(n2), np.arange(n2)) % n2
    th = 2.0 * np.pi * kn / n2
    c2, s2 = np.cos(th) / math.sqrt(n2), np.sin(th) / math.sqrt(n2)
    f2 = np.concatenate([np.concatenate([c2, s2], axis=1),
                         np.concatenate([-s2, c2], axis=1)], axis=0)
    cn = np.outer(np.arange(FGROUP), np.arange(FGROUP)) % FGROUP
    th = 2.0 * np.pi * cn / FGROUP
    cs = np.concatenate([np.cos(th), np.sin(th)], axis=0) / math.sqrt(FGROUP)
    return f1.astype(np.float32), f2.astype(np.float32), cs.astype(np.float32)


N_DR = 2 * WIN_ROWS - 1
N_DC = 2 * WIN_COLS - 1


def _bias_kernel(rpb_ref, o_ref):
    h = pl.program_id(0)
    c = lax.broadcasted_iota(jnp.int32, (GRID_W, HEAD_PAIR), 0)
    w = lax.broadcasted_iota(jnp.int32, (GRID_W, HEAD_PAIR), 1) % GRID_W
    diff = w - c + (WIN_COLS - 1)
    cs = jnp.clip(c - WIN_COLS // 2, 0, GRID_W - WIN_COLS)
    valid = (w >= cs) & (w < cs + WIN_COLS)
    for dr in range(N_DR):
        base = (h * N_DR + dr) * N_DC
        t = jnp.full((GRID_W, HEAD_PAIR), NEG_INF, F32)
        for dc in range(N_DC):
            t = jnp.where(diff == dc, rpb_ref[base + dc], t)
        t = jnp.where(valid, t, NEG_INF)
        for d in range(WIN_ROWS):
            a = dr - (WIN_ROWS - 1) + d
            if 0 <= a < WIN_ROWS:
                half = (a % 2) * GRID_W
                o_ref[0, d, :, a * GRID_W:(a + 1) * GRID_W] = t[:, half:half + GRID_W]


def _bias_tables(rpb_l):
    return pl.pallas_call(
        _bias_kernel,
        grid=(N_HEADS,),
        in_specs=[pl.BlockSpec(memory_space=pltpu.SMEM)],
        out_specs=pl.BlockSpec((1, WIN_ROWS, GRID_W, KEY_ROWS), lambda h: (h // 2, 0, h % 2, 0)),
        out_shape=jax.ShapeDtypeStruct((N_HEADS // 2, WIN_ROWS, HEAD_PAIR, KEY_ROWS), F32),
        compiler_params=pltpu.CompilerParams(dimension_semantics=("parallel",)),
        name="bias_expand",
    )(rpb_l.astype(F32).reshape(-1))


def _in_proj_kernel(x_ref, w_ref, u_ref, q_ref, k_ref, v_ref):
    xb = x_ref[...].astype(BF16)
    lo = 0
    for o_ref, scale in ((u_ref, None), (q_ref, ATT_SCALE), (k_ref, None), (v_ref, None)):
        width = o_ref.shape[-1]
        h = jnp.dot(xb, w_ref[:, lo:lo + width], preferred_element_type=F32)
        if scale is not None:
            h = h * scale
        o_ref[...] = h.astype(BF16)
        lo += width


def _in_proj(x2, w_in_b):
    t = x2.shape[0]
    tm = TM_PROJ
    widths = (FOURIER_WIDTH, ATT_WIDTH, ATT_WIDTH, ATT_WIDTH)
    return pl.pallas_call(
        _in_proj_kernel,
        grid=(t // tm,),
        in_specs=[pl.BlockSpec((tm, D_MODEL), lambda i: (i, 0)),
                  pl.BlockSpec((D_MODEL, IN_WIDTH), lambda i: (0, 0))],
        out_specs=[pl.BlockSpec((tm, wd), lambda i: (i, 0)) for wd in widths],
        out_shape=[jax.ShapeDtypeStruct((t, wd), BF16) for wd in widths],
        compiler_params=pltpu.CompilerParams(
            dimension_semantics=("parallel",), vmem_limit_bytes=V7X_VMEM_LIMIT),
        name="in_proj",
    )(x2, w_in_b)


def _fourier_kernel(u_ref, f1_ref, f2_ref, cs_ref, o_ref, tok_ref, y_ref, *, n1, n2):
    grp = FOURIER_UNROLL
    tp = n2 + FOURIER_ROW_PAD
    yp = 2 * n1 + FOURIER_ROW_PAD

    def stage_in(i, carry):
        src = pl.multiple_of(i * n2, n2)
        dst = pl.multiple_of(i * tp, 8)
        tok_ref[pl.ds(dst, n2), :] = u_ref[0, pl.ds(src, n2), :].astype(F32)
        return carry

    lax.fori_loop(0, n1, stage_in, 0, unroll=grp)

    def stage1(nn, carry):
        xs = tok_ref[pl.ds(nn, n1, stride=tp), :].astype(BF16)
        z = jnp.dot(f1_ref[nn], xs, preferred_element_type=F32)
        y_ref[pl.ds(pl.multiple_of(nn * yp, 8), 2 * n1), :] = z
        return carry

    lax.fori_loop(0, n2, stage1, 0, unroll=grp)

    def stage2(t, carry):
        zs = []
        for j in range(grp):
            k1 = t * grp + j
            zr = y_ref[pl.ds(k1, n2, stride=yp), :]
            zi = y_ref[pl.ds(n1 + k1, n2, stride=yp), :]
            zs.append(jnp.concatenate([zr, zi], axis=0).astype(BF16))
        xs = []
        for j in range(0, grp, 2):
            z2 = jnp.concatenate([zs[j], zs[j + 1]], axis=1)
            x2 = jnp.dot(f2_ref[...], z2, preferred_element_type=F32).astype(BF16)
            xs.extend([x2[:, :FGROUP], x2[:, FGROUP:]])
        for j, x in enumerate(xs):
            k1 = t * grp + j
            xc = jnp.concatenate([x[:n2], x[n2:]], axis=1)
            o = jnp.dot(xc, cs_ref[...], preferred_element_type=F32)
            tok_ref[pl.ds(pl.multiple_of(k1 * tp, 8), n2), :] = o
        return carry

    lax.fori_loop(0, n1 // grp, stage2, 0)

    def stage_out(k2, carry):
        o = tok_ref[pl.ds(k2, n1, stride=tp), :]
        o_ref[0, pl.ds(pl.multiple_of(k2 * n1, n1), n1), :] = o.astype(BF16)
        return carry

    lax.fori_loop(0, n2, stage_out, 0, unroll=grp)


def _fourier_mix(u3, f1_b, f2_b, cs_b):
    b, s, _ = u3.shape
    n1 = DFT_N1
    n2 = s // n1
    kern = functools.partial(_fourier_kernel, n1=n1, n2=n2)
    return pl.pallas_call(
        kern,
        grid=(b, N_FGROUPS),
        in_specs=[pl.BlockSpec((1, s, FGROUP), lambda bi, g: (bi, 0, g)),
                  pl.BlockSpec((n2, 2 * n1, n1), lambda bi, g: (0, 0, 0),
                               pipeline_mode=pl.Buffered(1)),
                  pl.BlockSpec((2 * n2, 2 * n2), lambda bi, g: (0, 0)),
                  pl.BlockSpec((2 * FGROUP, FGROUP), lambda bi, g: (0, 0))],
        out_specs=pl.BlockSpec((1, s, FGROUP), lambda bi, g: (bi, 0, g)),
        out_shape=jax.ShapeDtypeStruct((b, s, FOURIER_WIDTH), BF16),
        scratch_shapes=[pltpu.VMEM((n1 * (n2 + FOURIER_ROW_PAD), FGROUP), F32),
                        pltpu.VMEM((n2 * (2 * n1 + FOURIER_ROW_PAD), FGROUP), F32)],
        compiler_params=pltpu.CompilerParams(
            dimension_semantics=("parallel", "parallel"), vmem_limit_bytes=V7X_VMEM_LIMIT),
        name="fourier_mix",
    )(u3, f1_b, f2_b, cs_b)


def _attn_kernel(q_ref, k_ref, v_ref, bias_ref, o_ref, *, rows, rq):
    rc = pl.program_id(2)
    lane = lax.broadcasted_iota(jnp.int32, (GRID_W, HEAD_PAIR), 1)
    first_head = lane < HEAD_DIM

    def scores(g):
        out = []
        for j in range(ATT_GROUP):
            i = g * ATT_GROUP + j
            r = rc * rq + i
            rs = jnp.clip(r - WIN_ROWS // 2, 0, rows - WIN_ROWS)
            k0 = pl.multiple_of(rs * GRID_W, GRID_W)
            q2 = q_ref[0, i * GRID_W:(i + 1) * GRID_W, :]
            zero = jnp.zeros_like(q2)
            qs = jnp.concatenate([jnp.where(first_head, q2, zero),
                                  jnp.where(first_head, zero, q2)], axis=0)
            kk = k_ref[0, pl.ds(k0, KEY_ROWS), :]
            s = lax.dot_general(qs, kk, (((1,), (1,)), ((), ())), preferred_element_type=F32)
            out.append((i, k0, s + bias_ref[0, r - rs]))
        return out

    def finish(group):
        probs = []
        for i, k0, s in group:
            m = jnp.max(s, axis=-1, keepdims=True)
            p = jnp.exp(s - m)
            probs.append((i, k0, p.astype(BF16), jnp.sum(p, axis=-1, keepdims=True)))
        for i, k0, p, l in probs:
            vv = v_ref[0, pl.ds(k0, KEY_ROWS), :]
            o = jnp.dot(p, vv, preferred_element_type=F32) / l
            o2 = jnp.where(first_head, o[:GRID_W], o[GRID_W:])
            o_ref[0, i * GRID_W:(i + 1) * GRID_W, :] = o2.astype(BF16)

    n_groups = rq // ATT_GROUP
    pending = [scores(g) for g in range(min(ATT_AHEAD, n_groups))]
    for g in range(n_groups):
        if g + ATT_AHEAD < n_groups:
            pending.append(scores(g + ATT_AHEAD))
        finish(pending.pop(0))


def _attention(q3, k3, v3, bias):
    b, s, _ = q3.shape
    rows = s // GRID_W
    rq = ATT_ROWS
    nhp = N_HEADS // 2
    kern = functools.partial(_attn_kernel, rows=rows, rq=rq)
    return pl.pallas_call(
        kern,
        grid=(b, nhp, rows // rq),
        in_specs=[pl.BlockSpec((1, rq * GRID_W, HEAD_PAIR), lambda bi, hp, rc: (bi, rc, hp)),
                  pl.BlockSpec((1, s, HEAD_PAIR), lambda bi, hp, rc: (bi, 0, hp)),
                  pl.BlockSpec((1, s, HEAD_PAIR), lambda bi, hp, rc: (bi, 0, hp)),
                  pl.BlockSpec((1, WIN_ROWS, HEAD_PAIR, KEY_ROWS), lambda bi, hp, rc: (hp, 0, 0, 0))],
        out_specs=pl.BlockSpec((1, rq * GRID_W, HEAD_PAIR), lambda bi, hp, rc: (bi, rc, hp)),
        out_shape=jax.ShapeDtypeStruct((b, s, ATT_WIDTH), BF16),
        compiler_params=pltpu.CompilerParams(
            dimension_semantics=("parallel", "parallel", "arbitrary"),
            vmem_limit_bytes=V7X_VMEM_LIMIT),
        name="nbr_attention",
    )(q3, k3, v3, bias)


def _out_proj_kernel(f_ref, a_ref, x_ref, wf_ref, wa_ref, g_ref, b_ref, o_ref):
    mix = jnp.dot(f_ref[...], wf_ref[...], preferred_element_type=F32)
    mix = mix + jnp.dot(a_ref[...], wa_ref[...], preferred_element_type=F32)
    z = ALPHA * x_ref[...] + mix
    o_ref[...] = _layer_norm(z, g_ref[...], b_ref[...])


def _out_proj(f2d, a2d, x2, w_out_b, g, bta):
    t = x2.shape[0]
    tm = TM_PROJ
    const = lambda i: (0, 0)
    return pl.pallas_call(
        _out_proj_kernel,
        grid=(t // tm,),
        in_specs=[pl.BlockSpec((tm, FOURIER_WIDTH), lambda i: (i, 0)),
                  pl.BlockSpec((tm, ATT_WIDTH), lambda i: (i, 0)),
                  pl.BlockSpec((tm, D_MODEL), lambda i: (i, 0)),
                  pl.BlockSpec((FOURIER_WIDTH, D_MODEL), lambda i: (0, 0)),
                  pl.BlockSpec((ATT_WIDTH, D_MODEL), lambda i: (1, 0)),
                  pl.BlockSpec((1, D_MODEL), const),
                  pl.BlockSpec((1, D_MODEL), const)],
        out_specs=pl.BlockSpec((tm, D_MODEL), lambda i: (i, 0)),
        out_shape=jax.ShapeDtypeStruct((t, D_MODEL), F32),
        compiler_params=pltpu.CompilerParams(
            dimension_semantics=("parallel",), vmem_limit_bytes=V7X_VMEM_LIMIT),
        name="out_proj_ln",
    )(f2d, a2d, x2, w_out_b, w_out_b, g, bta)


def _ffn_kernel(x_ref, wg_ref, wu_ref, wd_ref, g_ref, b_ref, o_ref, acc_ref):
    x = x_ref[...]
    xb = x.astype(BF16)
    for c in range(D_FF // FF_CHUNK):
        sl = slice(c * FF_CHUNK, (c + 1) * FF_CHUNK)
        gate = jnp.dot(xb, wg_ref[:, sl], preferred_element_type=F32)
        up = jnp.dot(xb, wu_ref[:, sl], preferred_element_type=F32)
        hid = (gate * jax.nn.sigmoid(gate) * up).astype(BF16)
        part = jnp.dot(hid, wd_ref[sl, :], preferred_element_type=F32)
        if c == 0:
            acc_ref[...] = part
        else:
            acc_ref[...] += part
    z = ALPHA * x + acc_ref[...]
    o_ref[...] = _layer_norm(z, g_ref[...], b_ref[...])


def _ffn(x2, wg_b, wu_b, wd_b, g, bta):
    t = x2.shape[0]
    tm = TM_PROJ
    const = lambda i: (0, 0)
    return pl.pallas_call(
        _ffn_kernel,
        grid=(t // tm,),
        in_specs=[pl.BlockSpec((tm, D_MODEL), lambda i: (i, 0)),
                  pl.BlockSpec((D_MODEL, D_FF), const),
                  pl.BlockSpec((D_MODEL, D_FF), const),
                  pl.BlockSpec((D_FF, D_MODEL), const),
                  pl.BlockSpec((1, D_MODEL), const),
                  pl.BlockSpec((1, D_MODEL), const)],
        out_specs=pl.BlockSpec((tm, D_MODEL), lambda i: (i, 0)),
        out_shape=jax.ShapeDtypeStruct((t, D_MODEL), F32),
        scratch_shapes=[pltpu.VMEM((tm, D_MODEL), F32)],
        compiler_params=pltpu.CompilerParams(
            dimension_semantics=("parallel",), vmem_limit_bytes=V7X_VMEM_LIMIT),
        name="ffn_ln",
    )(x2, wg_b, wu_b, wd_b, g, bta)


def _layer(x3, p):
    b, s, d = x3.shape
    f1, f2, cs = _dft_constants(s)
    f1_b = jnp.asarray(f1).astype(BF16)
    f2_b = jnp.asarray(f2).astype(BF16)
    cs_b = jnp.asarray(cs).astype(BF16)

    x2d = x3.reshape(b * s, d)
    u, q, k, v = _in_proj(x2d, p["w_in"])
    fmix = _fourier_mix(u.reshape(b, s, FOURIER_WIDTH), f1_b, f2_b, cs_b)
    att = _attention(q.reshape(b, s, ATT_WIDTH), k.reshape(b, s, ATT_WIDTH),
                     v.reshape(b, s, ATT_WIDTH), p["bias"])
    x1 = _out_proj(fmix.reshape(b * s, FOURIER_WIDTH), att.reshape(b * s, ATT_WIDTH), x2d,
                   p["w_out"], p["ln1_g"], p["ln1_b"])
    x2 = _ffn(x1, p["w_gate"], p["w_up"], p["w_down"], p["ln2_g"], p["ln2_b"])
    return x2.reshape(b, s, d)


def kernel(x_prompt, x_sample, w_in, w_out, rpb, ln1_g, ln1_b, w_gate, w_up, w_down, ln2_g, ln2_b):
    y_prompt = x_prompt
    y_sample = x_sample
    for l in range(DEPTH):
        p = dict(
            w_in=w_in[l].astype(BF16), w_out=w_out[l].astype(BF16),
            w_gate=w_gate[l].astype(BF16), w_up=w_up[l].astype(BF16), w_down=w_down[l].astype(BF16),
            bias=_bias_tables(rpb[l]),
            ln1_g=ln1_g[l].reshape(1, D_MODEL), ln1_b=ln1_b[l].reshape(1, D_MODEL),
            ln2_g=ln2_g[l].reshape(1, D_MODEL), ln2_b=ln2_b[l].reshape(1, D_MODEL),
        )
        y_prompt = _layer(y_prompt, p)
        y_sample = _layer(y_sample, p)
    return (y_prompt, y_sample)
```

```python
import functools
import math

import jax
import jax.numpy as jnp
import numpy as np
from jax import lax
from jax.experimental import pallas as pl
from jax.experimental.pallas import tpu as pltpu

F32 = jnp.float32
BF16 = jnp.bfloat16

D_MODEL = 1024
DEPTH = 2
FOURIER_WIDTH = 512
FGROUP = 128
N_FGROUPS = FOURIER_WIDTH // FGROUP
ATT_WIDTH = 512
HEAD_DIM = 64
N_HEADS = ATT_WIDTH // HEAD_DIM
IN_WIDTH = FOURIER_WIDTH + 3 * ATT_WIDTH
GRID_W = 64
WIN_ROWS = 8
WIN_COLS = 16
D_FF = 2816
ALPHA = (2 * DEPTH) ** 0.25
LN_EPS = 1e-5
NEG_INF = -1e30
ATT_SCALE = HEAD_DIM ** -0.5

DFT_N1 = 128
HEAD_PAIR = 2 * HEAD_DIM
KEY_ROWS = WIN_ROWS * GRID_W

V7X_VMEM_LIMIT = 56 * 1024 * 1024

TM_PROJ = 512
TM_POST = 1024
POST_SUB = 512
FF_CHUNK = 256
FOURIER_ROW_PAD = 8
FOURIER_UNROLL = 16
ATT_ROWS = 32
ATT_GROUP = 1
ATT_AHEAD = 3


def _layer_norm(z, g, b):
    mu = jnp.mean(z, axis=-1, keepdims=True)
    zc = z - mu
    var = jnp.mean(zc * zc, axis=-1, keepdims=True)
    return zc * lax.rsqrt(var + LN_EPS) * g + b


@functools.lru_cache(maxsize=None)
def _dft_constants(seq):
    n1 = DFT_N1
    n2 = seq // n1
    n = np.arange(n1)[None, None, :] * n2 + np.arange(n2)[:, None, None]
    m = (np.arange(n1)[None, :, None] * n) % seq
    th = 2.0 * np.pi * m / seq
    f1 = np.concatenate([np.cos(th), -np.sin(th)], axis=1) / math.sqrt(n1)
    kn = np.outer(np.arange(n2), np.arange(n2)) % n2
    th = 2.0 * np.pi * kn / n2
    c2, s2 = np.cos(th) / math.sqrt(n2), np.sin(th) / math.sqrt(n2)
    f2 = np.concatenate([np.concatenate([c2, s2], axis=1),
                         np.concatenate([-s2, c2], axis=1)], axis=0)
    cn = np.outer(np.arange(FGROUP), np.arange(FGROUP)) % FGROUP
    th = 2.0 * np.pi * cn / FGROUP
    cs = np.concatenate([np.cos(th), np.sin(th)], axis=0) / math.sqrt(FGROUP)
    return f1.astype(np.float32), f2.astype(np.float32), cs.astype(np.float32)


N_DR = 2 * WIN_ROWS - 1
N_DC = 2 * WIN_COLS - 1


def _bias_kernel(rpb_ref, o_ref):
    h = pl.program_id(0)
    c = lax.broadcasted_iota(jnp.int32, (GRID_W, HEAD_PAIR), 0)
    w = lax.broadcasted_iota(jnp.int32, (GRID_W, HEAD_PAIR), 1) % GRID_W
    diff = w - c + (WIN_COLS - 1)
    cs = jnp.clip(c - WIN_COLS // 2, 0, GRID_W - WIN_COLS)
    valid = (w >= cs) & (w < cs + WIN_COLS)
    for dr in range(N_DR):
        base = (h * N_DR + dr) * N_DC
        t = jnp.full((GRID_W, HEAD_PAIR), NEG_INF, F32)
        for dc in range(N_DC):
            t = jnp.where(diff == dc, rpb_ref[base + dc], t)
        t = jnp.where(valid, t, NEG_INF)
        for d in range(WIN_ROWS):
            a = dr - (WIN_ROWS - 1) + d
            if 0 <= a < WIN_ROWS:
                half = (a % 2) * GRID_W
                o_ref[0, d, :, a * GRID_W:(a + 1) * GRID_W] = t[:, half:half + GRID_W]


def _bias_tables(rpb_l):
    return pl.pallas_call(
        _bias_kernel,
        grid=(N_HEADS,),
        in_specs=[pl.BlockSpec(memory_space=pltpu.SMEM)],
        out_specs=pl.BlockSpec((1, WIN_ROWS, GRID_W, KEY_ROWS), lambda h: (h // 2, 0, h % 2, 0)),
        out_shape=jax.ShapeDtypeStruct((N_HEADS // 2, WIN_ROWS, HEAD_PAIR, KEY_ROWS), F32),
        compiler_params=pltpu.CompilerParams(dimension_semantics=("parallel",)),
        name="bias_expand",
    )(rpb_l.astype(F32).reshape(-1))


def _in_proj_kernel(x_ref, w_ref, u_ref, q_ref, k_ref, v_ref):
    xb = x_ref[...].astype(BF16)
    lo = 0
    for o_ref, scale in ((u_ref, None), (q_ref, ATT_SCALE), (k_ref, None), (v_ref, None)):
        width = o_ref.shape[-1]
        h = jnp.dot(xb, w_ref[:, lo:lo + width], preferred_element_type=F32)
        if scale is not None:
            h = h * scale
        o_ref[...] = h.astype(BF16)
        lo += width


def _in_proj(x2, w_in_b):
    t = x2.shape[0]
    tm = TM_PROJ
    widths = (FOURIER_WIDTH, ATT_WIDTH, ATT_WIDTH, ATT_WIDTH)
    return pl.pallas_call(
        _in_proj_kernel,
        grid=(t // tm,),
        in_specs=[pl.BlockSpec((tm, D_MODEL), lambda i: (i, 0)),
                  pl.BlockSpec((D_MODEL, IN_WIDTH), lambda i: (0, 0))],
        out_specs=[pl.BlockSpec((tm, wd), lambda i: (i, 0)) for wd in widths],
        out_shape=[jax.ShapeDtypeStruct((t, wd), BF16) for wd in widths],
        compiler_params=pltpu.CompilerParams(
            dimension_semantics=("parallel",), vmem_limit_bytes=V7X_VMEM_LIMIT),
        name="in_proj",
    )(x2, w_in_b)


def _fourier_kernel(u_ref, f1_ref, f2_ref, cs_ref, o_ref, tok_ref, y_ref, *, n1, n2):
    grp = FOURIER_UNROLL
    tp = n2 + FOURIER_ROW_PAD
    yp = 2 * n1 + FOURIER_ROW_PAD

    def stage_in(i, carry):
        src = pl.multiple_of(i * n2, n2)
        dst = pl.multiple_of(i * tp, 8)
        tok_ref[pl.ds(dst, n2), :] = u_ref[0, pl.ds(src, n2), :].astype(F32)
        return carry

    lax.fori_loop(0, n1, stage_in, 0, unroll=grp)

    def stage1(nn, carry):
        xs = tok_ref[pl.ds(nn, n1, stride=tp), :].astype(BF16)
        z = jnp.dot(f1_ref[nn], xs, preferred_element_type=F32)
        y_ref[pl.ds(pl.multiple_of(nn * yp, 8), 2 * n1), :] = z
        return carry

    lax.fori_loop(0, n2, stage1, 0, unroll=grp)

    def stage2(t, carry):
        zs = []
        for j in range(grp):
            k1 = t * grp + j
            zr = y_ref[pl.ds(k1, n2, stride=yp), :]
            zi = y_ref[pl.ds(n1 + k1, n2, stride=yp), :]
            zs.append(jnp.concatenate([zr, zi], axis=0).astype(BF16))
        xs = []
        for j in range(0, grp, 2):
            z2 = jnp.concatenate([zs[j], zs[j + 1]], axis=1)
            x2 = jnp.dot(f2_ref[...], z2, preferred_element_type=F32).astype(BF16)
            xs.extend([x2[:, :FGROUP], x2[:, FGROUP:]])
        for j, x in enumerate(xs):
            k1 = t * grp + j
            xc = jnp.concatenate([x[:n2], x[n2:]], axis=1)
            o = jnp.dot(xc, cs_ref[...], preferred_element_type=F32)
            tok_ref[pl.ds(pl.multiple_of(k1 * tp, 8), n2), :] = o
        return carry

    lax.fori_loop(0, n1 // grp, stage2, 0)

    def stage_out(k2, carry):
        o = tok_ref[pl.ds(k2, n1, stride=tp), :]
        o_ref[0, pl.ds(pl.multiple_of(k2 * n1, n1), n1), :] = o.astype(BF16)
        return carry

    lax.fori_loop(0, n2, stage_out, 0, unroll=grp)


def _fourier_mix(u3, f1_b, f2_b, cs_b):
    b, s, _ = u3.shape
    n1 = DFT_N1
    n2 = s // n1
    kern = functools.partial(_fourier_kernel, n1=n1, n2=n2)
    return pl.pallas_call(
        kern,
        grid=(b, N_FGROUPS),
        in_specs=[pl.BlockSpec((1, s, FGROUP), lambda bi, g: (bi, 0, g)),
                  pl.BlockSpec((n2, 2 * n1, n1), lambda bi, g: (0, 0, 0),
                               pipeline_mode=pl.Buffered(1)),
                  pl.BlockSpec((2 * n2, 2 * n2), lambda bi, g: (0, 0)),
                  pl.BlockSpec((2 * FGROUP, FGROUP), lambda bi, g: (0, 0))],
        out_specs=pl.BlockSpec((1, s, FGROUP), lambda bi, g: (bi, 0, g)),
        out_shape=jax.ShapeDtypeStruct((b, s, FOURIER_WIDTH), BF16),
        scratch_shapes=[pltpu.VMEM((n1 * (n2 + FOURIER_ROW_PAD), FGROUP), F32),
                        pltpu.VMEM((n2 * (2 * n1 + FOURIER_ROW_PAD), FGROUP), F32)],
        compiler_params=pltpu.CompilerParams(
            dimension_semantics=("parallel", "parallel"), vmem_limit_bytes=V7X_VMEM_LIMIT),
        name="fourier_mix",
    )(u3, f1_b, f2_b, cs_b)


def _attn_kernel(q_ref, k_ref, v_ref, bias_ref, o_ref, *, rows, rq):
    rc = pl.program_id(2)
    lane = lax.broadcasted_iota(jnp.int32, (GRID_W, HEAD_PAIR), 1)
    first_head = lane < HEAD_DIM

    def scores(g):
        out = []
        for j in range(ATT_GROUP):
            i = g * ATT_GROUP + j
            r = rc * rq + i
            rs = jnp.clip(r - WIN_ROWS // 2, 0, rows - WIN_ROWS)
            k0 = pl.multiple_of(rs * GRID_W, GRID_W)
            q2 = q_ref[0, i * GRID_W:(i + 1) * GRID_W, :]
            zero = jnp.zeros_like(q2)
            qs = jnp.concatenate([jnp.where(first_head, q2, zero),
                                  jnp.where(first_head, zero, q2)], axis=0)
            kk = k_ref[0, pl.ds(k0, KEY_ROWS), :]
            s = lax.dot_general(qs, kk, (((1,), (1,)), ((), ())), preferred_element_type=F32)
            out.append((i, k0, s + bias_ref[0, r - rs]))
        return out

    def finish(group):
        probs = []
        for i, k0, s in group:
            m = jnp.max(s, axis=-1, keepdims=True)
            p = jnp.exp(s - m)
            probs.append((i, k0, p.astype(BF16), jnp.sum(p, axis=-1, keepdims=True)))
        for i, k0, p, l in probs:
            vv = v_ref[0, pl.ds(k0, KEY_ROWS), :]
            o = jnp.dot(p, vv, preferred_element_type=F32) / l
            o2 = jnp.where(first_head, o[:GRID_W], o[GRID_W:])
            o_ref[0, i * GRID_W:(i + 1) * GRID_W, :] = o2.astype(BF16)

    n_groups = rq // ATT_GROUP
    pending = [scores(g) for g in range(min(ATT_AHEAD, n_groups))]
    for g in range(n_groups):
        if g + ATT_AHEAD < n_groups:
            pending.append(scores(g + ATT_AHEAD))
        finish(pending.pop(0))


def _attention(q3, k3, v3, bias):
    b, s, _ = q3.shape
    rows = s // GRID_W
    rq = ATT_ROWS
    nhp = N_HEADS // 2
    kern = functools.partial(_attn_kernel, rows=rows, rq=rq)
    return pl.pallas_call(
        kern,
        grid=(b, nhp, rows // rq),
        in_specs=[pl.BlockSpec((1, rq * GRID_W, HEAD_PAIR), lambda bi, hp, rc: (bi, rc, hp)),
                  pl.BlockSpec((1, s, HEAD_PAIR), lambda bi, hp, rc: (bi, 0, hp)),
                  pl.BlockSpec((1, s, HEAD_PAIR), lambda bi, hp, rc: (bi, 0, hp)),
                  pl.BlockSpec((1, WIN_ROWS, HEAD_PAIR, KEY_ROWS), lambda bi, hp, rc: (hp, 0, 0, 0))],
        out_specs=pl.BlockSpec((1, rq * GRID_W, HEAD_PAIR), lambda bi, hp, rc: (bi, rc, hp)),
        out_shape=jax.ShapeDtypeStruct((b, s, ATT_WIDTH), BF16),
        compiler_params=pltpu.CompilerParams(
            dimension_semantics=("parallel", "parallel", "arbitrary"),
            vmem_limit_bytes=V7X_VMEM_LIMIT),
        name="nbr_attention",
    )(q3, k3, v3, bias)


def _post_kernel(f_ref, a_ref, x_ref, wf_ref, wa_ref, g1_ref, b1_ref, wg_ref, wu_ref, wd_ref,
                 g2_ref, b2_ref, o_ref, x1_ref, x1b_ref, acc_ref):
    subs = [slice(r, r + POST_SUB) for r in range(0, x_ref.shape[0], POST_SUB)]
    for rows in subs:
        mix = jnp.dot(f_ref[rows, :], wf_ref[...], preferred_element_type=F32)
        mix = mix + jnp.dot(a_ref[rows, :], wa_ref[...], preferred_element_type=F32)
        x1 = _layer_norm(ALPHA * x_ref[rows, :] + mix, g1_ref[...], b1_ref[...])
        x1_ref[rows, :] = x1
        x1b_ref[rows, :] = x1.astype(BF16)
    for rows in subs:
        xb = x1b_ref[rows, :]
        for c in range(D_FF // FF_CHUNK):
            sl = slice(c * FF_CHUNK, (c + 1) * FF_CHUNK)
            gate = jnp.dot(xb, wg_ref[:, sl], preferred_element_type=F32)
            up = jnp.dot(xb, wu_ref[:, sl], preferred_element_type=F32)
            hid = (gate * jax.nn.sigmoid(gate) * up).astype(BF16)
            part = jnp.dot(hid, wd_ref[sl, :], preferred_element_type=F32)
            if c == 0:
                acc_ref[rows, :] = part
            else:
                acc_ref[rows, :] += part
        z = ALPHA * x1_ref[rows, :] + acc_ref[rows, :]
        o_ref[rows, :] = _layer_norm(z, g2_ref[...], b2_ref[...])


def _post(f2d, a2d, x2, p):
    t = x2.shape[0]
    tm = TM_POST
    const = lambda i: (0, 0)
    resident = pl.Buffered(1)
    return pl.pallas_call(
        _post_kernel,
        grid=(t // tm,),
        in_specs=[pl.BlockSpec((tm, FOURIER_WIDTH), lambda i: (i, 0)),
                  pl.BlockSpec((tm, ATT_WIDTH), lambda i: (i, 0)),
                  pl.BlockSpec((tm, D_MODEL), lambda i: (i, 0)),
                  pl.BlockSpec((FOURIER_WIDTH, D_MODEL), lambda i: (0, 0), pipeline_mode=resident),
                  pl.BlockSpec((ATT_WIDTH, D_MODEL), lambda i: (1, 0), pipeline_mode=resident),
                  pl.BlockSpec((1, D_MODEL), const),
                  pl.BlockSpec((1, D_MODEL), const),
                  pl.BlockSpec((D_MODEL, D_FF), const, pipeline_mode=resident),
                  pl.BlockSpec((D_MODEL, D_FF), const, pipeline_mode=resident),
                  pl.BlockSpec((D_FF, D_MODEL), const, pipeline_mode=resident),
                  pl.BlockSpec((1, D_MODEL), const),
                  pl.BlockSpec((1, D_MODEL), const)],
        out_specs=pl.BlockSpec((tm, D_MODEL), lambda i: (i, 0)),
        out_shape=jax.ShapeDtypeStruct((t, D_MODEL), F32),
        scratch_shapes=[pltpu.VMEM((tm, D_MODEL), F32), pltpu.VMEM((tm, D_MODEL), BF16),
                        pltpu.VMEM((tm, D_MODEL), F32)],
        compiler_params=pltpu.CompilerParams(
            dimension_semantics=("parallel",), vmem_limit_bytes=V7X_VMEM_LIMIT),
        name="post_ffn",
    )(f2d, a2d, x2, p["w_out"], p["w_out"], p["ln1_g"], p["ln1_b"],
      p["w_gate"], p["w_up"], p["w_down"], p["ln2_g"], p["ln2_b"])


def _layer(x3, p):
    b, s, d = x3.shape
    f1, f2, cs = _dft_constants(s)
    f1_b = jnp.asarray(f1).astype(BF16)
    f2_b = jnp.asarray(f2).astype(BF16)
    cs_b = jnp.asarray(cs).astype(BF16)

    x2d = x3.reshape(b * s, d)
    u, q, k, v = _in_proj(x2d, p["w_in"])
    fmix = _fourier_mix(u.reshape(b, s, FOURIER_WIDTH), f1_b, f2_b, cs_b)
    att = _attention(q.reshape(b, s, ATT_WIDTH), k.reshape(b, s, ATT_WIDTH),
                     v.reshape(b, s, ATT_WIDTH), p["bias"])
    x2 = _post(fmix.reshape(b * s, FOURIER_WIDTH), att.reshape(b * s, ATT_WIDTH), x2d, p)
    return x2.reshape(b, s, d)


def kernel(x_prompt, x_sample, w_in, w_out, rpb, ln1_g, ln1_b, w_gate, w_up, w_down, ln2_g, ln2_b):
    y_prompt = x_prompt
    y_sample = x_sample
    for l in range(DEPTH):
        p = dict(
            w_in=w_in[l].astype(BF16), w_out=w_out[l].astype(BF16),
            w_gate=w_gate[l].astype(BF16), w_up=w_up[l].astype(BF16), w_down=w_down[l].astype(BF16),
            bias=_bias_tables(rpb[l]),
            ln1_g=ln1_g[l].reshape(1, D_MODEL), ln1_b=ln1_b[l].reshape(1, D_MODEL),
            ln2_g=ln2_g[l].reshape(1, D_MODEL), ln2_b=ln2_b[l].reshape(1, D_MODEL),
        )
        y_prompt = _layer(y_prompt, p)
        y_sample = _layer(y_sample, p)
    return (y_prompt, y_sample)
```

```python
import functools
import math

import jax
import jax.numpy as jnp
import numpy as np
from jax import lax
from jax.experimental import pallas as pl
from jax.experimental.pallas import tpu as pltpu

F32 = jnp.float32
BF16 = jnp.bfloat16

D_MODEL = 1024
DEPTH = 2
FOURIER_WIDTH = 512
FGROUP = 128
N_FGROUPS = FOURIER_WIDTH // FGROUP
ATT_WIDTH = 512
HEAD_DIM = 64
N_HEADS = ATT_WIDTH // HEAD_DIM
IN_WIDTH = FOURIER_WIDTH + 3 * ATT_WIDTH
GRID_W = 64
WIN_ROWS = 8
WIN_COLS = 16
D_FF = 2816
ALPHA = (2 * DEPTH) ** 0.25
LN_EPS = 1e-5
NEG_INF = -1e30
ATT_SCALE = HEAD_DIM ** -0.5

DFT_N1 = 128
HEAD_PAIR = 2 * HEAD_DIM
KEY_ROWS = WIN_ROWS * GRID_W

V7X_VMEM_LIMIT = 56 * 1024 * 1024

TM_PROJ = 1024
TM_POST = 1024
POST_SUB = 512
FF_CHUNK = 256
FOURIER_ROW_PAD = 8
FOURIER_UNROLL = 16
ATT_ROWS = 64
ATT_GROUP = 1
ATT_AHEAD = 3


def _layer_norm(z, g, b):
    mu = jnp.mean(z, axis=-1, keepdims=True)
    zc = z - mu
    var = jnp.mean(zc * zc, axis=-1, keepdims=True)
    return zc * lax.rsqrt(var + LN_EPS) * g + b


@functools.lru_cache(maxsize=None)
def _dft_constants(seq):
    n1 = DFT_N1
    n2 = seq // n1
    n = np.arange(n1)[None, None, :] * n2 + np.arange(n2)[:, None, None]
    m = (np.arange(n1)[None, :, None] * n) % seq
    th = 2.0 * np.pi * m / seq
    f1 = np.concatenate([np.cos(th), -np.sin(th)], axis=1) / math.sqrt(n1)
    kn = np.outer(np.arange(n2), np.arange(n2)) % n2
    th = 2.0 * np.pi * kn / n2
    c2, s2 = np.cos(th) / math.sqrt(n2), np.sin(th) / math.sqrt(n2)
    f2 = np.concatenate([np.concatenate([c2, s2], axis=1),
                         np.concatenate([-s2, c2], axis=1)], axis=0)
    cn = np.outer(np.arange(FGROUP), np.arange(FGROUP)) % FGROUP
    th = 2.0 * np.pi * cn / FGROUP
    cs = np.concatenate([np.cos(th), np.sin(th)], axis=0) / math.sqrt(FGROUP)
    return f1.astype(np.float32), f2.astype(np.float32), cs.astype(np.float32)


N_DR = 2 * WIN_ROWS - 1
N_DC = 2 * WIN_COLS - 1


def _bias_kernel(rpb_ref, o_ref, *, layer):
    h = pl.program_id(0)
    c = lax.broadcasted_iota(jnp.int32, (GRID_W, HEAD_PAIR), 0)
    w = lax.broadcasted_iota(jnp.int32, (GRID_W, HEAD_PAIR), 1) % GRID_W
    diff = w - c + (WIN_COLS - 1)
    cs = jnp.clip(c - WIN_COLS // 2, 0, GRID_W - WIN_COLS)
    valid = (w >= cs) & (w < cs + WIN_COLS)
    for dr in range(N_DR):
        base = ((layer * N_HEADS + h) * N_DR + dr) * N_DC
        t = jnp.full((GRID_W, HEAD_PAIR), NEG_INF, F32)
        for dc in range(N_DC):
            t = jnp.where(diff == dc, rpb_ref[base + dc], t)
        t = jnp.where(valid, t, NEG_INF)
        for d in range(WIN_ROWS):
            a = dr - (WIN_ROWS - 1) + d
            if 0 <= a < WIN_ROWS:
                half = (a % 2) * GRID_W
                o_ref[0, d, :, a * GRID_W:(a + 1) * GRID_W] = t[:, half:half + GRID_W]


def _bias_tables(rpb, layer):
    return pl.pallas_call(
        functools.partial(_bias_kernel, layer=layer),
        grid=(N_HEADS,),
        in_specs=[pl.BlockSpec(memory_space=pltpu.SMEM)],
        out_specs=pl.BlockSpec((1, WIN_ROWS, GRID_W, KEY_ROWS), lambda h: (h // 2, 0, h % 2, 0)),
        out_shape=jax.ShapeDtypeStruct((N_HEADS // 2, WIN_ROWS, HEAD_PAIR, KEY_ROWS), F32),
        compiler_params=pltpu.CompilerParams(dimension_semantics=("parallel",)),
        name="bias_expand",
    )(rpb.astype(F32).reshape(-1))


def _in_proj_kernel(x_ref, w_ref, u_ref, q_ref, k_ref, v_ref):
    xb = x_ref[...].astype(BF16)
    lo = 0
    for o_ref, scale in ((u_ref, None), (q_ref, ATT_SCALE), (k_ref, None), (v_ref, None)):
        width = o_ref.shape[-1]
        h = jnp.dot(xb, w_ref[:, lo:lo + width], preferred_element_type=F32)
        if scale is not None:
            h = h * scale
        o_ref[...] = h.astype(BF16)
        lo += width


def _in_proj(x2, w_in_b, layer):
    t = x2.shape[0]
    tm = TM_PROJ
    widths = (FOURIER_WIDTH, ATT_WIDTH, ATT_WIDTH, ATT_WIDTH)
    return pl.pallas_call(
        _in_proj_kernel,
        grid=(t // tm,),
        in_specs=[pl.BlockSpec((tm, D_MODEL), lambda i: (i, 0)),
                  pl.BlockSpec((None, D_MODEL, IN_WIDTH), lambda i: (layer, 0, 0))],
        out_specs=[pl.BlockSpec((tm, wd), lambda i: (i, 0)) for wd in widths],
        out_shape=[jax.ShapeDtypeStruct((t, wd), BF16) for wd in widths],
        compiler_params=pltpu.CompilerParams(
            dimension_semantics=("parallel",), vmem_limit_bytes=V7X_VMEM_LIMIT),
        name="in_proj",
    )(x2, w_in_b)


def _fourier_kernel(u_ref, f1_ref, f2_ref, cs_ref, o_ref, tok_ref, y_ref, *, n1, n2):
    grp = FOURIER_UNROLL
    tp = n2 + FOURIER_ROW_PAD
    yp = 2 * n1 + FOURIER_ROW_PAD

    def stage_in(i, carry):
        src = pl.multiple_of(i * n2, n2)
        dst = pl.multiple_of(i * tp, 8)
        tok_ref[pl.ds(dst, n2), :] = u_ref[0, pl.ds(src, n2), :].astype(F32)
        return carry

    lax.fori_loop(0, n1, stage_in, 0, unroll=grp)

    def stage1(nn, carry):
        xs = tok_ref[pl.ds(nn, n1, stride=tp), :].astype(BF16)
        z = jnp.dot(f1_ref[nn], xs, preferred_element_type=F32)
        y_ref[pl.ds(pl.multiple_of(nn * yp, 8), 2 * n1), :] = z
        return carry

    lax.fori_loop(0, n2, stage1, 0, unroll=grp)

    def stage2(t, carry):
        zs = []
        for j in range(grp):
            k1 = t * grp + j
            zr = y_ref[pl.ds(k1, n2, stride=yp), :]
            zi = y_ref[pl.ds(n1 + k1, n2, stride=yp), :]
            zs.append(jnp.concatenate([zr, zi], axis=0).astype(BF16))
        xs = []
        for j in range(0, grp, 2):
            z2 = jnp.concatenate([zs[j], zs[j + 1]], axis=1)
            x2 = jnp.dot(f2_ref[...], z2, preferred_element_type=F32).astype(BF16)
            xs.extend([x2[:, :FGROUP], x2[:, FGROUP:]])
        for j, x in enumerate(xs):
            k1 = t * grp + j
            xc = jnp.concatenate([x[:n2], x[n2:]], axis=1)
            o = jnp.dot(xc, cs_ref[...], preferred_element_type=F32)
            tok_ref[pl.ds(pl.multiple_of(k1 * tp, 8), n2), :] = o
        return carry

    lax.fori_loop(0, n1 // grp, stage2, 0)

    def stage_out(k2, carry):
        o = tok_ref[pl.ds(k2, n1, stride=tp), :]
        o_ref[0, pl.ds(pl.multiple_of(k2 * n1, n1), n1), :] = o.astype(BF16)
        return carry

    lax.fori_loop(0, n2, stage_out, 0, unroll=grp)


def _fourier_mix(u3, f1_b, f2_b, cs_b):
    b, s, _ = u3.shape
    n1 = DFT_N1
    n2 = s // n1
    kern = functools.partial(_fourier_kernel, n1=n1, n2=n2)
    return pl.pallas_call(
        kern,
        grid=(b, N_FGROUPS),
        in_specs=[pl.BlockSpec((1, s, FGROUP), lambda bi, g: (bi, 0, g)),
                  pl.BlockSpec((n2, 2 * n1, n1), lambda bi, g: (0, 0, 0),
                               pipeline_mode=pl.Buffered(1)),
                  pl.BlockSpec((2 * n2, 2 * n2), lambda bi, g: (0, 0)),
                  pl.BlockSpec((2 * FGROUP, FGROUP), lambda bi, g: (0, 0))],
        out_specs=pl.BlockSpec((1, s, FGROUP), lambda bi, g: (bi, 0, g)),
        out_shape=jax.ShapeDtypeStruct((b, s, FOURIER_WIDTH), BF16),
        scratch_shapes=[pltpu.VMEM((n1 * (n2 + FOURIER_ROW_PAD), FGROUP), F32),
                        pltpu.VMEM((n2 * (2 * n1 + FOURIER_ROW_PAD), FGROUP), F32)],
        compiler_params=pltpu.CompilerParams(
            dimension_semantics=("parallel", "parallel"), vmem_limit_bytes=V7X_VMEM_LIMIT),
        name="fourier_mix",
    )(u3, f1_b, f2_b, cs_b)


def _attn_kernel(q_ref, k_ref, v_ref, bias_ref, o_ref, *, rows, rq):
    rc = pl.program_id(2)
    lane = lax.broadcasted_iota(jnp.int32, (GRID_W, HEAD_PAIR), 1)
    first_head = lane < HEAD_DIM

    def scores(g):
        out = []
        for j in range(ATT_GROUP):
            i = g * ATT_GROUP + j
            r = rc * rq + i
            rs = jnp.clip(r - WIN_ROWS // 2, 0, rows - WIN_ROWS)
            k0 = pl.multiple_of(rs * GRID_W, GRID_W)
            q2 = q_ref[0, i * GRID_W:(i + 1) * GRID_W, :]
            zero = jnp.zeros_like(q2)
            qs = jnp.concatenate([jnp.where(first_head, q2, zero),
                                  jnp.where(first_head, zero, q2)], axis=0)
            kk = k_ref[0, pl.ds(k0, KEY_ROWS), :]
            s = lax.dot_general(qs, kk, (((1,), (1,)), ((), ())), preferred_element_type=F32)
            out.append((i, k0, s + bias_ref[0, r - rs]))
        return out

    def finish(group):
        probs = []
        for i, k0, s in group:
            m = jnp.max(s, axis=-1, keepdims=True)
            p = jnp.exp(s - m)
            probs.append((i, k0, p.astype(BF16), jnp.sum(p, axis=-1, keepdims=True)))
        for i, k0, p, l in probs:
            vv = v_ref[0, pl.ds(k0, KEY_ROWS), :]
            o = jnp.dot(p, vv, preferred_element_type=F32) / l
            o2 = jnp.where(first_head, o[:GRID_W], o[GRID_W:])
            o_ref[0, i * GRID_W:(i + 1) * GRID_W, :] = o2.astype(BF16)

    n_groups = rq // ATT_GROUP
    pending = [scores(g) for g in range(min(ATT_AHEAD, n_groups))]
    for g in range(n_groups):
        if g + ATT_AHEAD < n_groups:
            pending.append(scores(g + ATT_AHEAD))
        finish(pending.pop(0))


def _attention(q3, k3, v3, bias):
    b, s, _ = q3.shape
    rows = s // GRID_W
    rq = ATT_ROWS
    nhp = N_HEADS // 2
    kern = functools.partial(_attn_kernel, rows=rows, rq=rq)
    return pl.pallas_call(
        kern,
        grid=(b, nhp, rows // rq),
        in_specs=[pl.BlockSpec((1, rq * GRID_W, HEAD_PAIR), lambda bi, hp, rc: (bi, rc, hp)),
                  pl.BlockSpec((1, s, HEAD_PAIR), lambda bi, hp, rc: (bi, 0, hp)),
                  pl.BlockSpec((1, s, HEAD_PAIR), lambda bi, hp, rc: (bi, 0, hp)),
                  pl.BlockSpec((1, WIN_ROWS, HEAD_PAIR, KEY_ROWS), lambda bi, hp, rc: (hp, 0, 0, 0))],
        out_specs=pl.BlockSpec((1, rq * GRID_W, HEAD_PAIR), lambda bi, hp, rc: (bi, rc, hp)),
        out_shape=jax.ShapeDtypeStruct((b, s, ATT_WIDTH), BF16),
        compiler_params=pltpu.CompilerParams(
            dimension_semantics=("parallel", "parallel", "arbitrary"),
            vmem_limit_bytes=V7X_VMEM_LIMIT),
        name="nbr_attention",
    )(q3, k3, v3, bias)


def _post_kernel(f_ref, a_ref, x_ref, wf_ref, wa_ref, g1_ref, b1_ref, wg_ref, wu_ref, wd_ref,
                 g2_ref, b2_ref, o_ref, x1_ref, x1b_ref, acc_ref):
    subs = [slice(r, r + POST_SUB) for r in range(0, x_ref.shape[0], POST_SUB)]
    for rows in subs:
        mix = jnp.dot(f_ref[rows, :], wf_ref[...], preferred_element_type=F32)
        mix = mix + jnp.dot(a_ref[rows, :], wa_ref[...], preferred_element_type=F32)
        x1 = _layer_norm(ALPHA * x_ref[rows, :] + mix, g1_ref[...], b1_ref[...])
        x1_ref[rows, :] = x1
        x1b_ref[rows, :] = x1.astype(BF16)
    for rows in subs:
        xb = x1b_ref[rows, :]
        for c in range(D_FF // FF_CHUNK):
            sl = slice(c * FF_CHUNK, (c + 1) * FF_CHUNK)
            gate = jnp.dot(xb, wg_ref[:, sl], preferred_element_type=F32)
            up = jnp.dot(xb, wu_ref[:, sl], preferred_element_type=F32)
            hid = (gate * jax.nn.sigmoid(gate) * up).astype(BF16)
            part = jnp.dot(hid, wd_ref[sl, :], preferred_element_type=F32)
            if c == 0:
                acc_ref[rows, :] = part
            else:
                acc_ref[rows, :] += part
        z = ALPHA * x1_ref[rows, :] + acc_ref[rows, :]
        o_ref[rows, :] = _layer_norm(z, g2_ref[...], b2_ref[...])


def _post(f2d, a2d, x2, p, layer):
    t = x2.shape[0]
    tm = TM_POST
    row = pl.BlockSpec((None, 1, D_MODEL), lambda i: (layer, 0, 0))
    resident = pl.Buffered(1)
    return pl.pallas_call(
        _post_kernel,
        grid=(t // tm,),
        in_specs=[pl.BlockSpec((tm, FOURIER_WIDTH), lambda i: (i, 0)),
                  pl.BlockSpec((tm, ATT_WIDTH), lambda i: (i, 0)),
                  pl.BlockSpec((tm, D_MODEL), lambda i: (i, 0)),
                  pl.BlockSpec((None, FOURIER_WIDTH, D_MODEL), lambda i: (layer, 0, 0),
                               pipeline_mode=resident),
                  pl.BlockSpec((None, ATT_WIDTH, D_MODEL), lambda i: (layer, 1, 0),
                               pipeline_mode=resident),
                  row, row,
                  pl.BlockSpec((None, D_MODEL, D_FF), lambda i: (layer, 0, 0), pipeline_mode=resident),
                  pl.BlockSpec((None, D_MODEL, D_FF), lambda i: (layer, 0, 0), pipeline_mode=resident),
                  pl.BlockSpec((None, D_FF, D_MODEL), lambda i: (layer, 0, 0), pipeline_mode=resident),
                  row, row],
        out_specs=pl.BlockSpec((tm, D_MODEL), lambda i: (i, 0)),
        out_shape=jax.ShapeDtypeStruct((t, D_MODEL), F32),
        scratch_shapes=[pltpu.VMEM((tm, D_MODEL), F32), pltpu.VMEM((tm, D_MODEL), BF16),
                        pltpu.VMEM((tm, D_MODEL), F32)],
        compiler_params=pltpu.CompilerParams(
            dimension_semantics=("parallel",), vmem_limit_bytes=V7X_VMEM_LIMIT),
        name="post_ffn",
    )(f2d, a2d, x2, p["w_out"], p["w_out"], p["ln1_g"], p["ln1_b"],
      p["w_gate"], p["w_up"], p["w_down"], p["ln2_g"], p["ln2_b"])


def _layer(x3, p, layer):
    b, s, d = x3.shape
    f1, f2, cs = _dft_constants(s)
    f1_b = jnp.asarray(f1).astype(BF16)
    f2_b = jnp.asarray(f2).astype(BF16)
    cs_b = jnp.asarray(cs).astype(BF16)

    x2d = x3.reshape(b * s, d)
    u, q, k, v = _in_proj(x2d, p["w_in"], layer)
    fmix = _fourier_mix(u.reshape(b, s, FOURIER_WIDTH), f1_b, f2_b, cs_b)
    att = _attention(q.reshape(b, s, ATT_WIDTH), k.reshape(b, s, ATT_WIDTH),
                     v.reshape(b, s, ATT_WIDTH), p["bias"])
    x2 = _post(fmix.reshape(b * s, FOURIER_WIDTH), att.reshape(b * s, ATT_WIDTH), x2d, p, layer)
    return x2.reshape(b, s, d)


def kernel(x_prompt, x_sample, w_in, w_out, rpb, ln1_g, ln1_b, w_gate, w_up, w_down, ln2_g, ln2_b):
    p = dict(
        w_in=w_in.astype(BF16), w_out=w_out.astype(BF16),
        w_gate=w_gate.astype(BF16), w_up=w_up.astype(BF16), w_down=w_down.astype(BF16),
        ln1_g=ln1_g.reshape(DEPTH, 1, D_MODEL), ln1_b=ln1_b.reshape(DEPTH, 1, D_MODEL),
        ln2_g=ln2_g.reshape(DEPTH, 1, D_MODEL), ln2_b=ln2_b.reshape(DEPTH, 1, D_MODEL),
    )
    y_prompt = x_prompt
    y_sample = x_sample
    for layer in range(DEPTH):
        p["bias"] = _bias_tables(rpb, layer)
        y_prompt = _layer(y_prompt, p, layer)
        y_sample = _layer(y_sample, p, layer)
    return (y_prompt, y_sample)
```

```python
import functools
import math

import jax
import jax.numpy as jnp
import numpy as np
from jax import lax
from jax.experimental import pallas as pl
from jax.experimental.pallas import tpu as pltpu

F32 = jnp.float32
BF16 = jnp.bfloat16

D_MODEL = 1024
DEPTH = 2
FOURIER_WIDTH = 512
FGROUP = 128
N_FGROUPS = FOURIER_WIDTH // FGROUP
ATT_WIDTH = 512
HEAD_DIM = 64
N_HEADS = ATT_WIDTH // HEAD_DIM
IN_WIDTH = FOURIER_WIDTH + 3 * ATT_WIDTH
GRID_W = 64
WIN_ROWS = 8
WIN_COLS = 16
D_FF = 2816
ALPHA = (2 * DEPTH) ** 0.25
LN_EPS = 1e-5
NEG_INF = -1e30
LOG2_E = math.log2(math.e)
ATT_SCALE = HEAD_DIM ** -0.5 * LOG2_E

DFT_N1 = 128
HEAD_PAIR = 2 * HEAD_DIM
KEY_ROWS = WIN_ROWS * GRID_W

V7X_VMEM_LIMIT = 56 * 1024 * 1024

TM_PROJ = 1024
TM_POST = 1024
POST_SUB = 512
FF_CHUNK = 256
FOURIER_ROW_PAD = 8
FOURIER_UNROLL = 16
ATT_ROWS = 64
ATT_GROUP = 1
ATT_AHEAD = 3


def _layer_norm(z, g, b):
    mu = jnp.mean(z, axis=-1, keepdims=True)
    zc = z - mu
    var = jnp.mean(zc * zc, axis=-1, keepdims=True)
    return zc * lax.rsqrt(var + LN_EPS) * g + b


@functools.lru_cache(maxsize=None)
def _dft_constants(seq):
    n1 = DFT_N1
    n2 = seq // n1
    n = np.arange(n1)[None, None, :] * n2 + np.arange(n2)[:, None, None]
    m = (np.arange(n1)[None, :, None] * n) % seq
    th = 2.0 * np.pi * m / seq
    f1 = np.concatenate([np.cos(th), -np.sin(th)], axis=1) / math.sqrt(n1)
    kn = np.outer(np.arange(n2), np.arange(n2)) % n2
    th = 2.0 * np.pi * kn / n2
    c2, s2 = np.cos(th) / math.sqrt(n2), np.sin(th) / math.sqrt(n2)
    f2 = np.concatenate([np.concatenate([c2, s2], axis=1),
                         np.concatenate([-s2, c2], axis=1)], axis=0)
    cn = np.outer(np.arange(FGROUP), np.arange(FGROUP)) % FGROUP
    th = 2.0 * np.pi * cn / FGROUP
    cs = np.concatenate([np.cos(th), np.sin(th)], axis=0) / math.sqrt(FGROUP)
    return f1.astype(np.float32), f2.astype(np.float32), cs.astype(np.float32)


N_DR = 2 * WIN_ROWS - 1
N_DC = 2 * WIN_COLS - 1


def _bias_kernel(rpb_ref, o_ref, *, layer):
    h = pl.program_id(0)
    c = lax.broadcasted_iota(jnp.int32, (GRID_W, HEAD_PAIR), 0)
    w = lax.broadcasted_iota(jnp.int32, (GRID_W, HEAD_PAIR), 1) % GRID_W
    diff = w - c + (WIN_COLS - 1)
    cs = jnp.clip(c - WIN_COLS // 2, 0, GRID_W - WIN_COLS)
    valid = (w >= cs) & (w < cs + WIN_COLS)
    lane_w = lax.broadcasted_iota(jnp.int32, (8, HEAD_PAIR), 1) % GRID_W
    for dr in range(N_DR):
        base = ((layer * N_HEADS + h) * N_DR + dr) * N_DC
        vec = jnp.zeros((8, HEAD_PAIR), F32)
        for dc in range(N_DC):
            vec = jnp.where(lane_w == dc, rpb_ref[base + dc] * LOG2_E, vec)
        t = pltpu.roll(jnp.concatenate([vec] * (GRID_W // 8), axis=0),
                       HEAD_PAIR - (WIN_COLS - 1), 1, stride=1, stride_axis=0)
        t = jnp.where(valid, t, NEG_INF)
        for d in range(WIN_ROWS):
            a = dr - (WIN_ROWS - 1) + d
            if 0 <= a < WIN_ROWS:
                half = (a % 2) * GRID_W
                o_ref[0, d, :, a * GRID_W:(a + 1) * GRID_W] = t[:, half:half + GRID_W]


def _bias_tables(rpb, layer):
    return pl.pallas_call(
        functools.partial(_bias_kernel, layer=layer),
        grid=(N_HEADS,),
        in_specs=[pl.BlockSpec(memory_space=pltpu.SMEM)],
        out_specs=pl.BlockSpec((1, WIN_ROWS, GRID_W, KEY_ROWS), lambda h: (h // 2, 0, h % 2, 0)),
        out_shape=jax.ShapeDtypeStruct((N_HEADS // 2, WIN_ROWS, HEAD_PAIR, KEY_ROWS), F32),
        compiler_params=pltpu.CompilerParams(dimension_semantics=("parallel",)),
        name="bias_expand",
    )(rpb.astype(F32).reshape(-1))


def _in_proj_kernel(x_ref, w_ref, u_ref, q_ref, k_ref, v_ref):
    xb = x_ref[...].astype(BF16)
    lo = 0
    for o_ref, scale in ((u_ref, None), (q_ref, ATT_SCALE), (k_ref, None), (v_ref, None)):
        width = o_ref.shape[-1]
        h = jnp.dot(xb, w_ref[:, lo:lo + width], preferred_element_type=F32)
        if scale is not None:
            h = h * scale
        o_ref[...] = h.astype(BF16)
        lo += width


def _in_proj(x2, w_in_b, layer):
    t = x2.shape[0]
    tm = TM_PROJ
    widths = (FOURIER_WIDTH, ATT_WIDTH, ATT_WIDTH, ATT_WIDTH)
    return pl.pallas_call(
        _in_proj_kernel,
        grid=(t // tm,),
        in_specs=[pl.BlockSpec((tm, D_MODEL), lambda i: (i, 0)),
                  pl.BlockSpec((None, D_MODEL, IN_WIDTH), lambda i: (layer, 0, 0))],
        out_specs=[pl.BlockSpec((tm, wd), lambda i: (i, 0)) for wd in widths],
        out_shape=[jax.ShapeDtypeStruct((t, wd), BF16) for wd in widths],
        compiler_params=pltpu.CompilerParams(
            dimension_semantics=("parallel",), vmem_limit_bytes=V7X_VMEM_LIMIT),
        name="in_proj",
    )(x2, w_in_b)


def _fourier_kernel(u_ref, f1_ref, f2_ref, cs_ref, o_ref, tok_ref, y_ref, *, n1, n2):
    grp = FOURIER_UNROLL
    tp = n2 + FOURIER_ROW_PAD
    yp = 2 * n1 + FOURIER_ROW_PAD

    def stage_in(i, carry):
        src = pl.multiple_of(i * n2, n2)
        dst = pl.multiple_of(i * tp, 8)
        tok_ref[pl.ds(dst, n2), :] = u_ref[0, pl.ds(src, n2), :].astype(F32)
        return carry

    lax.fori_loop(0, n1, stage_in, 0, unroll=grp)

    def stage1(nn, carry):
        xs = tok_ref[pl.ds(nn, n1, stride=tp), :].astype(BF16)
        z = jnp.dot(f1_ref[nn], xs, preferred_element_type=F32)
        y_ref[pl.ds(pl.multiple_of(nn * yp, 8), 2 * n1), :] = z
        return carry

    lax.fori_loop(0, n2, stage1, 0, unroll=grp)

    def stage2(t, carry):
        zs = []
        for j in range(grp):
            k1 = t * grp + j
            zr = y_ref[pl.ds(k1, n2, stride=yp), :]
            zi = y_ref[pl.ds(n1 + k1, n2, stride=yp), :]
            zs.append(jnp.concatenate([zr, zi], axis=0).astype(BF16))
        xs = []
        for j in range(0, grp, 2):
            z2 = jnp.concatenate([zs[j], zs[j + 1]], axis=1)
            x2 = jnp.dot(f2_ref[...], z2, preferred_element_type=F32).astype(BF16)
            xs.extend([x2[:, :FGROUP], x2[:, FGROUP:]])
        for j, x in enumerate(xs):
            k1 = t * grp + j
            xc = jnp.concatenate([x[:n2], x[n2:]], axis=1)
            o = jnp.dot(xc, cs_ref[...], preferred_element_type=F32)
            tok_ref[pl.ds(pl.multiple_of(k1 * tp, 8), n2), :] = o
        return carry

    lax.fori_loop(0, n1 // grp, stage2, 0)

    def stage_out(k2, carry):
        o = tok_ref[pl.ds(k2, n1, stride=tp), :]
        o_ref[0, pl.ds(pl.multiple_of(k2 * n1, n1), n1), :] = o.astype(BF16)
        return carry

    lax.fori_loop(0, n2, stage_out, 0, unroll=grp)


def _fourier_mix(u3, f1_b, f2_b, cs_b):
    b, s, _ = u3.shape
    n1 = DFT_N1
    n2 = s // n1
    kern = functools.partial(_fourier_kernel, n1=n1, n2=n2)
    return pl.pallas_call(
        kern,
        grid=(b, N_FGROUPS),
        in_specs=[pl.BlockSpec((1, s, FGROUP), lambda bi, g: (bi, 0, g)),
                  pl.BlockSpec((n2, 2 * n1, n1), lambda bi, g: (0, 0, 0),
                               pipeline_mode=pl.Buffered(1)),
                  pl.BlockSpec((2 * n2, 2 * n2), lambda bi, g: (0, 0)),
                  pl.BlockSpec((2 * FGROUP, FGROUP), lambda bi, g: (0, 0))],
        out_specs=pl.BlockSpec((1, s, FGROUP), lambda bi, g: (bi, 0, g)),
        out_shape=jax.ShapeDtypeStruct((b, s, FOURIER_WIDTH), BF16),
        scratch_shapes=[pltpu.VMEM((n1 * (n2 + FOURIER_ROW_PAD), FGROUP), F32),
                        pltpu.VMEM((n2 * (2 * n1 + FOURIER_ROW_PAD), FGROUP), F32)],
        compiler_params=pltpu.CompilerParams(
            dimension_semantics=("parallel", "parallel"), vmem_limit_bytes=V7X_VMEM_LIMIT),
        name="fourier_mix",
    )(u3, f1_b, f2_b, cs_b)


def _attn_kernel(q_ref, k_ref, v_ref, bias_ref, o_ref, *, rows, rq):
    rc = pl.program_id(2)
    lane = lax.broadcasted_iota(jnp.int32, (GRID_W, HEAD_PAIR), 1)
    first_head = lane < HEAD_DIM

    def scores(g):
        out = []
        for j in range(ATT_GROUP):
            i = g * ATT_GROUP + j
            r = rc * rq + i
            rs = jnp.clip(r - WIN_ROWS // 2, 0, rows - WIN_ROWS)
            k0 = pl.multiple_of(rs * GRID_W, GRID_W)
            q2 = q_ref[0, i * GRID_W:(i + 1) * GRID_W, :]
            zero = jnp.zeros_like(q2)
            qs = jnp.concatenate([jnp.where(first_head, q2, zero),
                                  jnp.where(first_head, zero, q2)], axis=0)
            kk = k_ref[0, pl.ds(k0, KEY_ROWS), :]
            s = lax.dot_general(qs, kk, (((1,), (1,)), ((), ())), preferred_element_type=F32)
            out.append((i, k0, s + bias_ref[0, r - rs]))
        return out

    def finish(group):
        probs = []
        for i, k0, s in group:
            m = jnp.max(s, axis=-1, keepdims=True)
            p = jnp.exp2(s - m)
            probs.append((i, k0, p.astype(BF16), jnp.sum(p, axis=-1, keepdims=True)))
        for i, k0, p, l in probs:
            vv = v_ref[0, pl.ds(k0, KEY_ROWS), :]
            o = jnp.dot(p, vv, preferred_element_type=F32) / l
            o2 = jnp.where(first_head, o[:GRID_W], o[GRID_W:])
            o_ref[0, i * GRID_W:(i + 1) * GRID_W, :] = o2.astype(BF16)

    n_groups = rq // ATT_GROUP
    pending = [scores(g) for g in range(min(ATT_AHEAD, n_groups))]
    for g in range(n_groups):
        if g + ATT_AHEAD < n_groups:
            pending.append(scores(g + ATT_AHEAD))
        finish(pending.pop(0))


def _attention(q3, k3, v3, bias):
    b, s, _ = q3.shape
    rows = s // GRID_W
    rq = ATT_ROWS
    assert s % GRID_W == 0 and rows % rq == 0 and rows >= WIN_ROWS, (s, rq)
    nhp = N_HEADS // 2
    kern = functools.partial(_attn_kernel, rows=rows, rq=rq)
    return pl.pallas_call(
        kern,
        grid=(b, nhp, rows // rq),
        in_specs=[pl.BlockSpec((1, rq * GRID_W, HEAD_PAIR), lambda bi, hp, rc: (bi, rc, hp)),
                  pl.BlockSpec((1, s, HEAD_PAIR), lambda bi, hp, rc: (bi, 0, hp)),
                  pl.BlockSpec((1, s, HEAD_PAIR), lambda bi, hp, rc: (bi, 0, hp)),
                  pl.BlockSpec((1, WIN_ROWS, HEAD_PAIR, KEY_ROWS), lambda bi, hp, rc: (hp, 0, 0, 0))],
        out_specs=pl.BlockSpec((1, rq * GRID_W, HEAD_PAIR), lambda bi, hp, rc: (bi, rc, hp)),
        out_shape=jax.ShapeDtypeStruct((b, s, ATT_WIDTH), BF16),
        compiler_params=pltpu.CompilerParams(
            dimension_semantics=("parallel", "parallel", "arbitrary"),
            vmem_limit_bytes=V7X_VMEM_LIMIT),
        name="nbr_attention",
    )(q3, k3, v3, bias)


def _post_kernel(f_ref, a_ref, x_ref, wf_ref, wa_ref, g1_ref, b1_ref, wg_ref, wu_ref, wd_ref,
                 g2_ref, b2_ref, o_ref, x1_ref, x1b_ref, acc_ref):
    subs = [slice(r, r + POST_SUB) for r in range(0, x_ref.shape[0], POST_SUB)]
    for rows in subs:
        mix = jnp.dot(f_ref[rows, :], wf_ref[...], preferred_element_type=F32)
        mix = mix + jnp.dot(a_ref[rows, :], wa_ref[...], preferred_element_type=F32)
        x1 = _layer_norm(ALPHA * x_ref[rows, :] + mix, g1_ref[...], b1_ref[...])
        x1_ref[rows, :] = x1
        x1b_ref[rows, :] = x1.astype(BF16)
    for rows in subs:
        xb = x1b_ref[rows, :]
        for c, lo in enumerate(range(0, D_FF, FF_CHUNK)):
            sl = slice(lo, min(lo + FF_CHUNK, D_FF))
            gate = jnp.dot(xb, wg_ref[:, sl], preferred_element_type=F32)
            up = jnp.dot(xb, wu_ref[:, sl], preferred_element_type=F32)
            hid = (gate * jax.nn.sigmoid(gate) * up).astype(BF16)
            part = jnp.dot(hid, wd_ref[sl, :], preferred_element_type=F32)
            if c == 0:
                acc_ref[rows, :] = part
            else:
                acc_ref[rows, :] += part
        z = ALPHA * x1_ref[rows, :] + acc_ref[rows, :]
        o_ref[rows, :] = _layer_norm(z, g2_ref[...], b2_ref[...])


def _post(f2d, a2d, x2, p, layer):
    t = x2.shape[0]
    tm = TM_POST
    row = pl.BlockSpec((None, 1, D_MODEL), lambda i: (layer, 0, 0))
    resident = pl.Buffered(1)
    return pl.pallas_call(
        _post_kernel,
        grid=(t // tm,),
        in_specs=[pl.BlockSpec((tm, FOURIER_WIDTH), lambda i: (i, 0)),
                  pl.BlockSpec((tm, ATT_WIDTH), lambda i: (i, 0)),
                  pl.BlockSpec((tm, D_MODEL), lambda i: (i, 0)),
                  pl.BlockSpec((None, FOURIER_WIDTH, D_MODEL), lambda i: (layer, 0, 0),
                               pipeline_mode=resident),
                  pl.BlockSpec((None, ATT_WIDTH, D_MODEL), lambda i: (layer, 1, 0),
                               pipeline_mode=resident),
                  row, row,
                  pl.BlockSpec((None, D_MODEL, D_FF), lambda i: (layer, 0, 0), pipeline_mode=resident),
                  pl.BlockSpec((None, D_MODEL, D_FF), lambda i: (layer, 0, 0), pipeline_mode=resident),
                  pl.BlockSpec((None, D_FF, D_MODEL), lambda i: (layer, 0, 0), pipeline_mode=resident),
                  row, row],
        out_specs=pl.BlockSpec((tm, D_MODEL), lambda i: (i, 0)),
        out_shape=jax.ShapeDtypeStruct((t, D_MODEL), F32),
        scratch_shapes=[pltpu.VMEM((tm, D_MODEL), F32), pltpu.VMEM((tm, D_MODEL), BF16),
                        pltpu.VMEM((tm, D_MODEL), F32)],
        compiler_params=pltpu.CompilerParams(
            dimension_semantics=("parallel",), vmem_limit_bytes=V7X_VMEM_LIMIT),
        name="post_ffn",
    )(f2d, a2d, x2, p["w_out"], p["w_out"], p["ln1_g"], p["ln1_b"],
      p["w_gate"], p["w_up"], p["w_down"], p["ln2_g"], p["ln2_b"])


def _layer(x3, p, layer):
    b, s, d = x3.shape
    f1, f2, cs = _dft_constants(s)
    f1_b = jnp.asarray(f1).astype(BF16)
    f2_b = jnp.asarray(f2).astype(BF16)
    cs_b = jnp.asarray(cs).astype(BF16)

    x2d = x3.reshape(b * s, d)
    u, q, k, v = _in_proj(x2d, p["w_in"], layer)
    fmix = _fourier_mix(u.reshape(b, s, FOURIER_WIDTH), f1_b, f2_b, cs_b)
    att = _attention(q.reshape(b, s, ATT_WIDTH), k.reshape(b, s, ATT_WIDTH),
                     v.reshape(b, s, ATT_WIDTH), p["bias"])
    x2 = _post(fmix.reshape(b * s, FOURIER_WIDTH), att.reshape(b * s, ATT_WIDTH), x2d, p, layer)
    return x2.reshape(b, s, d)


def kernel(x_prompt, x_sample, w_in, w_out, rpb, ln1_g, ln1_b, w_gate, w_up, w_down, ln2_g, ln2_b):
    p = dict(
        w_in=w_in.astype(BF16), w_out=w_out.astype(BF16),
        w_gate=w_gate.astype(BF16), w_up=w_up.astype(BF16), w_down=w_down.astype(BF16),
        ln1_g=ln1_g.reshape(DEPTH, 1, D_MODEL), ln1_b=ln1_b.reshape(DEPTH, 1, D_MODEL),
        ln2_g=ln2_g.reshape(DEPTH, 1, D_MODEL), ln2_b=ln2_b.reshape(DEPTH, 1, D_MODEL),
    )
    y_prompt = x_prompt
    y_sample = x_sample
    for layer in range(DEPTH):
        p["bias"] = _bias_tables(rpb, layer)
        y_prompt = _layer(y_prompt, p, layer)
        y_sample = _layer(y_sample, p, layer)
    return (y_prompt, y_sample)
```

```python
import functools
import math

import jax
import jax.numpy as jnp
import numpy as np
from jax import lax
from jax.experimental import pallas as pl
from jax.experimental.pallas import tpu as pltpu

F32 = jnp.float32
BF16 = jnp.bfloat16

D_MODEL = 1024
DEPTH = 2
FOURIER_WIDTH = 512
FGROUP = 128
N_FGROUPS = FOURIER_WIDTH // FGROUP
ATT_WIDTH = 512
HEAD_DIM = 64
N_HEADS = ATT_WIDTH // HEAD_DIM
IN_WIDTH = FOURIER_WIDTH + 3 * ATT_WIDTH
GRID_W = 64
WIN_ROWS = 8
WIN_COLS = 16
D_FF = 2816
ALPHA = (2 * DEPTH) ** 0.25
LN_EPS = 1e-5
NEG_INF = -1e30
LOG2_E = math.log2(math.e)
ATT_SCALE = HEAD_DIM ** -0.5 * LOG2_E

DFT_N1 = 128
HEAD_PAIR = 2 * HEAD_DIM
KEY_ROWS = WIN_ROWS * GRID_W

V7X_VMEM_LIMIT = 56 * 1024 * 1024

TM_PROJ = 1024
TM_POST = 1024
POST_SUB = 512
FF_CHUNK = 256
FOURIER_ROW_PAD = 8
FOURIER_COPY_UNROLL = 64
FOURIER_UNROLL = 32
ATT_ROWS = 64
ATT_GROUP = 1
ATT_AHEAD = 3


def _layer_norm(z, g, b):
    mu = jnp.mean(z, axis=-1, keepdims=True)
    zc = z - mu
    var = jnp.mean(zc * zc, axis=-1, keepdims=True)
    return zc * lax.rsqrt(var + LN_EPS) * g + b


@functools.lru_cache(maxsize=None)
def _dft_constants(seq):
    n1 = DFT_N1
    n2 = seq // n1
    n = np.arange(n1)[None, None, :] * n2 + np.arange(n2)[:, None, None]
    m = (np.arange(n1)[None, :, None] * n) % seq
    th = 2.0 * np.pi * m / seq
    f1 = np.concatenate([np.cos(th), -np.sin(th)], axis=1) / math.sqrt(n1)
    kn = np.outer(np.arange(n2), np.arange(n2)) % n2
    th = 2.0 * np.pi * kn / n2
    c2, s2 = np.cos(th) / math.sqrt(n2), np.sin(th) / math.sqrt(n2)
    f2 = np.concatenate([np.concatenate([c2, s2], axis=1),
                         np.concatenate([-s2, c2], axis=1)], axis=0)
    cn = np.outer(np.arange(FGROUP), np.arange(FGROUP)) % FGROUP
    th = 2.0 * np.pi * cn / FGROUP
    cs = np.concatenate([np.cos(th), np.sin(th)], axis=0) / math.sqrt(FGROUP)
    return f1.astype(np.float32), f2.astype(np.float32), cs.astype(np.float32)


N_DR = 2 * WIN_ROWS - 1
N_DC = 2 * WIN_COLS - 1


def _bias_kernel(rpb_ref, o_ref, *, layer):
    h = pl.program_id(0)
    c = lax.broadcasted_iota(jnp.int32, (GRID_W, HEAD_PAIR), 0)
    w = lax.broadcasted_iota(jnp.int32, (GRID_W, HEAD_PAIR), 1) % GRID_W
    diff = w - c + (WIN_COLS - 1)
    cs = jnp.clip(c - WIN_COLS // 2, 0, GRID_W - WIN_COLS)
    valid = (w >= cs) & (w < cs + WIN_COLS)
    lane_w = lax.broadcasted_iota(jnp.int32, (8, HEAD_PAIR), 1) % GRID_W
    for dr in range(N_DR):
        base = ((layer * N_HEADS + h) * N_DR + dr) * N_DC
        vec = jnp.zeros((8, HEAD_PAIR), F32)
        for dc in range(N_DC):
            vec = jnp.where(lane_w == dc, rpb_ref[base + dc] * LOG2_E, vec)
        t = pltpu.roll(jnp.concatenate([vec] * (GRID_W // 8), axis=0),
                       HEAD_PAIR - (WIN_COLS - 1), 1, stride=1, stride_axis=0)
        t = jnp.where(valid, t, NEG_INF)
        for d in range(WIN_ROWS):
            a = dr - (WIN_ROWS - 1) + d
            if 0 <= a < WIN_ROWS:
                half = (a % 2) * GRID_W
                o_ref[0, d, :, a * GRID_W:(a + 1) * GRID_W] = t[:, half:half + GRID_W]


def _bias_tables(rpb, layer):
    return pl.pallas_call(
        functools.partial(_bias_kernel, layer=layer),
        grid=(N_HEADS,),
        in_specs=[pl.BlockSpec(memory_space=pltpu.SMEM)],
        out_specs=pl.BlockSpec((1, WIN_ROWS, GRID_W, KEY_ROWS), lambda h: (h // 2, 0, h % 2, 0)),
        out_shape=jax.ShapeDtypeStruct((N_HEADS // 2, WIN_ROWS, HEAD_PAIR, KEY_ROWS), F32),
        compiler_params=pltpu.CompilerParams(dimension_semantics=("parallel",)),
        name="bias_expand",
    )(rpb.astype(F32).reshape(-1))


def _in_proj_kernel(x_ref, w_ref, u_ref, q_ref, k_ref, v_ref):
    xb = x_ref[...].astype(BF16)
    lo = 0
    for o_ref, scale in ((u_ref, None), (q_ref, ATT_SCALE), (k_ref, None), (v_ref, None)):
        width = o_ref.shape[-1]
        h = jnp.dot(xb, w_ref[:, lo:lo + width], preferred_element_type=F32)
        if scale is not None:
            h = h * scale
        o_ref[...] = h.astype(BF16)
        lo += width


def _in_proj(x2, w_in_b, layer):
    t = x2.shape[0]
    tm = TM_PROJ
    widths = (FOURIER_WIDTH, ATT_WIDTH, ATT_WIDTH, ATT_WIDTH)
    return pl.pallas_call(
        _in_proj_kernel,
        grid=(t // tm,),
        in_specs=[pl.BlockSpec((tm, D_MODEL), lambda i: (i, 0)),
                  pl.BlockSpec((None, D_MODEL, IN_WIDTH), lambda i: (layer, 0, 0))],
        out_specs=[pl.BlockSpec((tm, wd), lambda i: (i, 0)) for wd in widths],
        out_shape=[jax.ShapeDtypeStruct((t, wd), BF16) for wd in widths],
        compiler_params=pltpu.CompilerParams(
            dimension_semantics=("parallel",), vmem_limit_bytes=V7X_VMEM_LIMIT),
        name="in_proj",
    )(x2, w_in_b)


def _fourier_kernel(u_ref, f1_ref, f2_ref, cs_ref, o_ref, tok_ref, y_ref, *, n1, n2):
    grp = FOURIER_UNROLL
    tp = n2 + FOURIER_ROW_PAD
    yp = 2 * n1 + FOURIER_ROW_PAD

    def stage_in(i, carry):
        src = pl.multiple_of(i * n2, n2)
        dst = pl.multiple_of(i * tp, 8)
        tok_ref[pl.ds(dst, n2), :] = u_ref[0, pl.ds(src, n2), :].astype(F32)
        return carry

    lax.fori_loop(0, n1, stage_in, 0, unroll=FOURIER_COPY_UNROLL)

    def stage1(nn, carry):
        xs = tok_ref[pl.ds(nn, n1, stride=tp), :].astype(BF16)
        z = jnp.dot(f1_ref[nn], xs, preferred_element_type=F32)
        y_ref[pl.ds(pl.multiple_of(nn * yp, 8), 2 * n1), :] = z
        return carry

    lax.fori_loop(0, n2, stage1, 0, unroll=FOURIER_COPY_UNROLL)

    def stage2(t, carry):
        zs = []
        for j in range(grp):
            k1 = t * grp + j
            zr = y_ref[pl.ds(k1, n2, stride=yp), :]
            zi = y_ref[pl.ds(n1 + k1, n2, stride=yp), :]
            zs.append(jnp.concatenate([zr, zi], axis=0).astype(BF16))
        xs = []
        for j in range(0, grp, 2):
            z2 = jnp.concatenate([zs[j], zs[j + 1]], axis=1)
            x2 = jnp.dot(f2_ref[...], z2, preferred_element_type=F32).astype(BF16)
            xs.extend([x2[:, :FGROUP], x2[:, FGROUP:]])
        for j, x in enumerate(xs):
            k1 = t * grp + j
            xc = jnp.concatenate([x[:n2], x[n2:]], axis=1)
            o = jnp.dot(xc, cs_ref[...], preferred_element_type=F32)
            tok_ref[pl.ds(pl.multiple_of(k1 * tp, 8), n2), :] = o
        return carry

    for t in range(n1 // grp):
        stage2(t, 0)

    def stage_out(k2, carry):
        o = tok_ref[pl.ds(k2, n1, stride=tp), :]
        o_ref[0, pl.ds(pl.multiple_of(k2 * n1, n1), n1), :] = o.astype(BF16)
        return carry

    lax.fori_loop(0, n2, stage_out, 0, unroll=FOURIER_COPY_UNROLL)


def _fourier_mix(u3, f1_b, f2_b, cs_b):
    b, s, _ = u3.shape
    n1 = DFT_N1
    n2 = s // n1
    kern = functools.partial(_fourier_kernel, n1=n1, n2=n2)
    return pl.pallas_call(
        kern,
        grid=(b, N_FGROUPS),
        in_specs=[pl.BlockSpec((1, s, FGROUP), lambda bi, g: (bi, 0, g)),
                  pl.BlockSpec((n2, 2 * n1, n1), lambda bi, g: (0, 0, 0),
                               pipeline_mode=pl.Buffered(1)),
                  pl.BlockSpec((2 * n2, 2 * n2), lambda bi, g: (0, 0)),
                  pl.BlockSpec((2 * FGROUP, FGROUP), lambda bi, g: (0, 0))],
        out_specs=pl.BlockSpec((1, s, FGROUP), lambda bi, g: (bi, 0, g)),
        out_shape=jax.ShapeDtypeStruct((b, s, FOURIER_WIDTH), BF16),
        scratch_shapes=[pltpu.VMEM((n1 * (n2 + FOURIER_ROW_PAD), FGROUP), F32),
                        pltpu.VMEM((n2 * (2 * n1 + FOURIER_ROW_PAD), FGROUP), F32)],
        compiler_params=pltpu.CompilerParams(
            dimension_semantics=("parallel", "parallel"), vmem_limit_bytes=V7X_VMEM_LIMIT),
        name="fourier_mix",
    )(u3, f1_b, f2_b, cs_b)


def _attn_kernel(q_ref, k_ref, v_ref, bias_ref, o_ref, *, rows, rq):
    rc = pl.program_id(2)
    lane = lax.broadcasted_iota(jnp.int32, (GRID_W, HEAD_PAIR), 1)
    first_head = lane < HEAD_DIM

    def scores(g):
        out = []
        for j in range(ATT_GROUP):
            i = g * ATT_GROUP + j
            r = rc * rq + i
            rs = jnp.clip(r - WIN_ROWS // 2, 0, rows - WIN_ROWS)
            k0 = pl.multiple_of(rs * GRID_W, GRID_W)
            q2 = q_ref[0, i * GRID_W:(i + 1) * GRID_W, :]
            zero = jnp.zeros_like(q2)
            qs = jnp.concatenate([jnp.where(first_head, q2, zero),
                                  jnp.where(first_head, zero, q2)], axis=0)
            kk = k_ref[0, pl.ds(k0, KEY_ROWS), :]
            s = lax.dot_general(qs, kk, (((1,), (1,)), ((), ())), preferred_element_type=F32)
            out.append((i, k0, s + bias_ref[0, r - rs]))
        return out

    def finish(group):
        probs = []
        for i, k0, s in group:
            m = jnp.max(s, axis=-1, keepdims=True)
            p = jnp.exp2(s - m)
            probs.append((i, k0, p.astype(BF16), jnp.sum(p, axis=-1, keepdims=True)))
        for i, k0, p, l in probs:
            vv = v_ref[0, pl.ds(k0, KEY_ROWS), :]
            o = jnp.dot(p, vv, preferred_element_type=F32) / l
            o2 = jnp.where(first_head, o[:GRID_W], o[GRID_W:])
            o_ref[0, i * GRID_W:(i + 1) * GRID_W, :] = o2.astype(BF16)

    n_groups = rq // ATT_GROUP
    pending = [scores(g) for g in range(min(ATT_AHEAD, n_groups))]
    for g in range(n_groups):
        if g + ATT_AHEAD < n_groups:
            pending.append(scores(g + ATT_AHEAD))
        finish(pending.pop(0))


def _attention(q3, k3, v3, bias):
    b, s, _ = q3.shape
    rows = s // GRID_W
    rq = ATT_ROWS
    assert s % GRID_W == 0 and rows % rq == 0 and rows >= WIN_ROWS, (s, rq)
    nhp = N_HEADS // 2
    kern = functools.partial(_attn_kernel, rows=rows, rq=rq)
    return pl.pallas_call(
        kern,
        grid=(b, nhp, rows // rq),
        in_specs=[pl.BlockSpec((1, rq * GRID_W, HEAD_PAIR), lambda bi, hp, rc: (bi, rc, hp)),
                  pl.BlockSpec((1, s, HEAD_PAIR), lambda bi, hp, rc: (bi, 0, hp)),
                  pl.BlockSpec((1, s, HEAD_PAIR), lambda bi, hp, rc: (bi, 0, hp)),
                  pl.BlockSpec((1, WIN_ROWS, HEAD_PAIR, KEY_ROWS), lambda bi, hp, rc: (hp, 0, 0, 0))],
        out_specs=pl.BlockSpec((1, rq * GRID_W, HEAD_PAIR), lambda bi, hp, rc: (bi, rc, hp)),
        out_shape=jax.ShapeDtypeStruct((b, s, ATT_WIDTH), BF16),
        compiler_params=pltpu.CompilerParams(
            dimension_semantics=("parallel", "parallel", "arbitrary"),
            vmem_limit_bytes=V7X_VMEM_LIMIT),
        name="nbr_attention",
    )(q3, k3, v3, bias)


def _post_kernel(f_ref, a_ref, x_ref, wf_ref, wa_ref, g1_ref, b1_ref, wg_ref, wu_ref, wd_ref,
                 g2_ref, b2_ref, o_ref, x1_ref, x1b_ref, acc_ref):
    subs = [slice(r, r + POST_SUB) for r in range(0, x_ref.shape[0], POST_SUB)]
    for rows in subs:
        mix = jnp.dot(f_ref[rows, :], wf_ref[...], preferred_element_type=F32)
        mix = mix + jnp.dot(a_ref[rows, :], wa_ref[...], preferred_element_type=F32)
        x1 = _layer_norm(ALPHA * x_ref[rows, :] + mix, g1_ref[...], b1_ref[...])
        x1_ref[rows, :] = x1
        x1b_ref[rows, :] = x1.astype(BF16)
    for rows in subs:
        xb = x1b_ref[rows, :]
        for c, lo in enumerate(range(0, D_FF, FF_CHUNK)):
            sl = slice(lo, lo + FF_CHUNK)
            gate = jnp.dot(xb, wg_ref[:, sl], preferred_element_type=F32)
            up = jnp.dot(xb, wu_ref[:, sl], preferred_element_type=F32)
            hid = (gate * jax.nn.sigmoid(gate) * up).astype(BF16)
            part = jnp.dot(hid, wd_ref[sl, :], preferred_element_type=F32)
            if c == 0:
                acc_ref[rows, :] = part
            else:
                acc_ref[rows, :] += part
        z = ALPHA * x1_ref[rows, :] + acc_ref[rows, :]
        o_ref[rows, :] = _layer_norm(z, g2_ref[...], b2_ref[...])


def _post(f2d, a2d, x2, p, layer):
    t = x2.shape[0]
    tm = TM_POST
    row = pl.BlockSpec((None, 1, D_MODEL), lambda i: (layer, 0, 0))
    resident = pl.Buffered(1)
    return pl.pallas_call(
        _post_kernel,
        grid=(t // tm,),
        in_specs=[pl.BlockSpec((tm, FOURIER_WIDTH), lambda i: (i, 0)),
                  pl.BlockSpec((tm, ATT_WIDTH), lambda i: (i, 0)),
                  pl.BlockSpec((tm, D_MODEL), lambda i: (i, 0)),
                  pl.BlockSpec((None, FOURIER_WIDTH, D_MODEL), lambda i: (layer, 0, 0),
                               pipeline_mode=resident),
                  pl.BlockSpec((None, ATT_WIDTH, D_MODEL), lambda i: (layer, 1, 0),
                               pipeline_mode=resident),
                  row, row,
                  pl.BlockSpec((None, D_MODEL, D_FF), lambda i: (layer, 0, 0), pipeline_mode=resident),
                  pl.BlockSpec((None, D_MODEL, D_FF), lambda i: (layer, 0, 0), pipeline_mode=resident),
                  pl.BlockSpec((None, D_FF, D_MODEL), lambda i: (layer, 0, 0), pipeline_mode=resident),
                  row, row],
        out_specs=pl.BlockSpec((tm, D_MODEL), lambda i: (i, 0)),
        out_shape=jax.ShapeDtypeStruct((t, D_MODEL), F32),
        scratch_shapes=[pltpu.VMEM((tm, D_MODEL), F32), pltpu.VMEM((tm, D_MODEL), BF16),
                        pltpu.VMEM((tm, D_MODEL), F32)],
        compiler_params=pltpu.CompilerParams(
            dimension_semantics=("parallel",), vmem_limit_bytes=V7X_VMEM_LIMIT),
        name="post_ffn",
    )(f2d, a2d, x2, p["w_out"], p["w_out"], p["ln1_g"], p["ln1_b"],
      p["w_gate"], p["w_up"], p["w_down"], p["ln2_g"], p["ln2_b"])


def _layer(x3, p, layer):
    b, s, d = x3.shape
    f1, f2, cs = _dft_constants(s)
    f1_b = jnp.asarray(f1).astype(BF16)
    f2_b = jnp.asarray(f2).astype(BF16)
    cs_b = jnp.asarray(cs).astype(BF16)

    x2d = x3.reshape(b * s, d)
    u, q, k, v = _in_proj(x2d, p["w_in"], layer)
    fmix = _fourier_mix(u.reshape(b, s, FOURIER_WIDTH), f1_b, f2_b, cs_b)
    att = _attention(q.reshape(b, s, ATT_WIDTH), k.reshape(b, s, ATT_WIDTH),
                     v.reshape(b, s, ATT_WIDTH), p["bias"])
    x2 = _post(fmix.reshape(b * s, FOURIER_WIDTH), att.reshape(b * s, ATT_WIDTH), x2d, p, layer)
    return x2.reshape(b, s, d)


def kernel(x_prompt, x_sample, w_in, w_out, rpb, ln1_g, ln1_b, w_gate, w_up, w_down, ln2_g, ln2_b):
    p = dict(
        w_in=w_in.astype(BF16), w_out=w_out.astype(BF16),
        w_gate=w_gate.astype(BF16), w_up=w_up.astype(BF16), w_down=w_down.astype(BF16),
        ln1_g=ln1_g.reshape(DEPTH, 1, D_MODEL), ln1_b=ln1_b.reshape(DEPTH, 1, D_MODEL),
        ln2_g=ln2_g.reshape(DEPTH, 1, D_MODEL), ln2_b=ln2_b.reshape(DEPTH, 1, D_MODEL),
    )
    y_prompt = x_prompt
    y_sample = x_sample
    for layer in range(DEPTH):
        p["bias"] = _bias_tables(rpb, layer)
        y_prompt = _layer(y_prompt, p, layer)
        y_sample = _layer(y_sample, p, layer)
    return (y_prompt, y_sample)
```

```python
import functools
import math

import jax
import jax.numpy as jnp
import numpy as np
from jax import lax
from jax.experimental import pallas as pl
from jax.experimental.pallas import tpu as pltpu

F32 = jnp.float32
BF16 = jnp.bfloat16

D_MODEL = 1024
DEPTH = 2
FOURIER_WIDTH = 512
FGROUP = 128
N_FGROUPS = FOURIER_WIDTH // FGROUP
ATT_WIDTH = 512
HEAD_DIM = 64
N_HEADS = ATT_WIDTH // HEAD_DIM
IN_WIDTH = FOURIER_WIDTH + 3 * ATT_WIDTH
GRID_W = 64
WIN_ROWS = 8
WIN_COLS = 16
D_FF = 2816
ALPHA = (2 * DEPTH) ** 0.25
LN_EPS = 1e-5
NEG_INF = -1e30
LOG2_E = math.log2(math.e)
ATT_SCALE = HEAD_DIM ** -0.5 * LOG2_E

DFT_N1 = 128
HEAD_PAIR = 2 * HEAD_DIM
KEY_ROWS = WIN_ROWS * GRID_W

V7X_VMEM_LIMIT = 56 * 1024 * 1024

TM_PROJ = 1024
TM_POST = 1024
POST_SUB = 512
FF_CHUNK = 256
FOURIER_ROW_PAD = 8
FOURIER_COPY_UNROLL = 64
FOURIER_UNROLL = 32
ATT_ROWS = 64
ATT_AHEAD = 3


def _layer_norm(z, g, b):
    mu = jnp.mean(z, axis=-1, keepdims=True)
    zc = z - mu
    var = jnp.mean(zc * zc, axis=-1, keepdims=True)
    return zc * lax.rsqrt(var + LN_EPS) * g + b


@functools.lru_cache(maxsize=None)
def _dft_constants(seq):
    n1 = DFT_N1
    n2 = seq // n1
    n = np.arange(n1)[None, None, :] * n2 + np.arange(n2)[:, None, None]
    m = (np.arange(n1)[None, :, None] * n) % seq
    th = 2.0 * np.pi * m / seq
    f1 = np.concatenate([np.cos(th), -np.sin(th)], axis=1) / math.sqrt(n1)
    kn = np.outer(np.arange(n2), np.arange(n2)) % n2
    th = 2.0 * np.pi * kn / n2
    c2, s2 = np.cos(th) / math.sqrt(n2), np.sin(th) / math.sqrt(n2)
    f2 = np.concatenate([np.concatenate([c2, s2], axis=1),
                         np.concatenate([-s2, c2], axis=1)], axis=0)
    cn = np.outer(np.arange(FGROUP), np.arange(FGROUP)) % FGROUP
    th = 2.0 * np.pi * cn / FGROUP
    cs = np.concatenate([np.cos(th), np.sin(th)], axis=0) / math.sqrt(FGROUP)
    return f1.astype(np.float32), f2.astype(np.float32), cs.astype(np.float32)


N_DR = 2 * WIN_ROWS - 1
N_DC = 2 * WIN_COLS - 1

BAND_W = 32
N_BANDS = GRID_W // BAND_W
BAND_KEYS = WIN_ROWS * BAND_W
QUERY_ALIGN = 16


def _band_query_ranges():
    c = np.arange(GRID_W)
    cs = np.clip(c - WIN_COLS // 2, 0, GRID_W - WIN_COLS)
    out = []
    for t in range(N_BANDS):
        sees = c[(cs < (t + 1) * BAND_W) & (cs + WIN_COLS > t * BAND_W)]
        lo = int(sees.min()) // QUERY_ALIGN * QUERY_ALIGN
        hi = -(-(int(sees.max()) + 1) // QUERY_ALIGN) * QUERY_ALIGN
        out.append((lo, hi))
    return out


BAND_Q_RANGE = _band_query_ranges()
BAND_Q = BAND_Q_RANGE[0][1] - BAND_Q_RANGE[0][0]
assert all(hi - lo == BAND_Q for lo, hi in BAND_Q_RANGE)


def _bias_kernel(rpb_ref, o_ref, *, layer):
    h = pl.program_id(0)
    c = lax.broadcasted_iota(jnp.int32, (GRID_W, HEAD_PAIR), 0)
    lane = lax.broadcasted_iota(jnp.int32, (GRID_W, HEAD_PAIR), 1)
    cs = jnp.clip(c - WIN_COLS // 2, 0, GRID_W - WIN_COLS)
    lane_w = lax.broadcasted_iota(jnp.int32, (8, HEAD_PAIR), 1) % GRID_W
    offsets = (0, BAND_W)
    valid = []
    for off in offsets:
        w = (lane - off) % GRID_W
        valid.append((w >= cs) & (w < cs + WIN_COLS))
    for dr in range(N_DR):
        base = ((layer * N_HEADS + h) * N_DR + dr) * N_DC
        vec = jnp.zeros((8, HEAD_PAIR), F32)
        for dc in range(N_DC):
            vec = jnp.where(lane_w == dc, rpb_ref[base + dc] * LOG2_E, vec)
        rep = jnp.concatenate([vec] * (GRID_W // 8), axis=0)
        tiles = []
        for off, ok in zip(offsets, valid):
            t_off = pltpu.roll(rep, (HEAD_PAIR - (WIN_COLS - 1) + off) % HEAD_PAIR, 1,
                               stride=1, stride_axis=0)
            tiles.append(jnp.where(ok, t_off, NEG_INF))
        for d in range(WIN_ROWS):
            a = dr - (WIN_ROWS - 1) + d
            if not 0 <= a < WIN_ROWS:
                continue
            dst = (a * BAND_W) % HEAD_PAIR
            for t, (lo, hi) in enumerate(BAND_Q_RANGE):
                tile = tiles[offsets.index((dst - t * BAND_W) % GRID_W)]
                o_ref[0, d, t, :, a * BAND_W:(a + 1) * BAND_W] = tile[lo:hi, dst:dst + BAND_W]


def _bias_tables(rpb, layer):
    return pl.pallas_call(
        functools.partial(_bias_kernel, layer=layer),
        grid=(N_HEADS,),
        in_specs=[pl.BlockSpec(memory_space=pltpu.SMEM)],
        out_specs=pl.BlockSpec((1, WIN_ROWS, N_BANDS, BAND_Q, BAND_KEYS),
                               lambda h: (h // 2, 0, 0, h % 2, 0)),
        out_shape=jax.ShapeDtypeStruct((N_HEADS // 2, WIN_ROWS, N_BANDS, 2 * BAND_Q, BAND_KEYS), F32),
        compiler_params=pltpu.CompilerParams(dimension_semantics=("parallel",)),
        name="bias_expand",
    )(rpb.astype(F32).reshape(-1))


def _in_proj_kernel(x_ref, w_ref, u_ref, q_ref, k_ref, v_ref):
    xb = x_ref[...].astype(BF16)
    lo = 0
    for o_ref, scale in ((u_ref, None), (q_ref, ATT_SCALE), (k_ref, None), (v_ref, None)):
        width = o_ref.shape[-1]
        h = jnp.dot(xb, w_ref[:, lo:lo + width], preferred_element_type=F32)
        if scale is not None:
            h = h * scale
        o_ref[...] = h.astype(BF16)
        lo += width


def _in_proj(x2, w_in_b, layer):
    t = x2.shape[0]
    tm = TM_PROJ
    widths = (FOURIER_WIDTH, ATT_WIDTH, ATT_WIDTH, ATT_WIDTH)
    return pl.pallas_call(
        _in_proj_kernel,
        grid=(t // tm,),
        in_specs=[pl.BlockSpec((tm, D_MODEL), lambda i: (i, 0)),
                  pl.BlockSpec((None, D_MODEL, IN_WIDTH), lambda i: (layer, 0, 0))],
        out_specs=[pl.BlockSpec((tm, wd), lambda i: (i, 0)) for wd in widths],
        out_shape=[jax.ShapeDtypeStruct((t, wd), BF16) for wd in widths],
        compiler_params=pltpu.CompilerParams(
            dimension_semantics=("parallel",), vmem_limit_bytes=V7X_VMEM_LIMIT),
        name="in_proj",
    )(x2, w_in_b)


def _fourier_kernel(u_ref, f1_ref, f2_ref, cs_ref, o_ref, tok_ref, y_ref, *, n1, n2):
    grp = FOURIER_UNROLL
    tp = n2 + FOURIER_ROW_PAD
    yp = 2 * n1 + FOURIER_ROW_PAD

    def stage_in(i, carry):
        src = pl.multiple_of(i * n2, n2)
        dst = pl.multiple_of(i * tp, 8)
        tok_ref[pl.ds(dst, n2), :] = u_ref[0, pl.ds(src, n2), :].astype(F32)
        return carry

    lax.fori_loop(0, n1, stage_in, 0, unroll=FOURIER_COPY_UNROLL)

    def stage1(nn, carry):
        xs = tok_ref[pl.ds(nn, n1, stride=tp), :].astype(BF16)
        z = jnp.dot(f1_ref[nn], xs, preferred_element_type=F32)
        y_ref[pl.ds(pl.multiple_of(nn * yp, 8), 2 * n1), :] = z
        return carry

    lax.fori_loop(0, n2, stage1, 0, unroll=FOURIER_COPY_UNROLL)

    def stage2(t, carry):
        zs = []
        for j in range(grp):
            k1 = t * grp + j
            zr = y_ref[pl.ds(k1, n2, stride=yp), :]
            zi = y_ref[pl.ds(n1 + k1, n2, stride=yp), :]
            zs.append(jnp.concatenate([zr, zi], axis=0).astype(BF16))
        xs = []
        for j in range(0, grp, 2):
            z2 = jnp.concatenate([zs[j], zs[j + 1]], axis=1)
            x2 = jnp.dot(f2_ref[...], z2, preferred_element_type=F32).astype(BF16)
            xs.extend([x2[:, :FGROUP], x2[:, FGROUP:]])
        for j, x in enumerate(xs):
            k1 = t * grp + j
            xc = jnp.concatenate([x[:n2], x[n2:]], axis=1)
            o = jnp.dot(xc, cs_ref[...], preferred_element_type=F32)
            tok_ref[pl.ds(pl.multiple_of(k1 * tp, 8), n2), :] = o
        return carry

    for t in range(n1 // grp):
        stage2(t, 0)

    def stage_out(k2, carry):
        o = tok_ref[pl.ds(k2, n1, stride=tp), :]
        o_ref[0, pl.ds(pl.multiple_of(k2 * n1, n1), n1), :] = o.astype(BF16)
        return carry

    lax.fori_loop(0, n2, stage_out, 0, unroll=FOURIER_COPY_UNROLL)


def _fourier_mix(u3, f1_b, f2_b, cs_b):
    b, s, _ = u3.shape
    n1 = DFT_N1
    n2 = s // n1
    kern = functools.partial(_fourier_kernel, n1=n1, n2=n2)
    return pl.pallas_call(
        kern,
        grid=(b, N_FGROUPS),
        in_specs=[pl.BlockSpec((1, s, FGROUP), lambda bi, g: (bi, 0, g)),
                  pl.BlockSpec((n2, 2 * n1, n1), lambda bi, g: (0, 0, 0),
                               pipeline_mode=pl.Buffered(1)),
                  pl.BlockSpec((2 * n2, 2 * n2), lambda bi, g: (0, 0)),
                  pl.BlockSpec((2 * FGROUP, FGROUP), lambda bi, g: (0, 0))],
        out_specs=pl.BlockSpec((1, s, FGROUP), lambda bi, g: (bi, 0, g)),
        out_shape=jax.ShapeDtypeStruct((b, s, FOURIER_WIDTH), BF16),
        scratch_shapes=[pltpu.VMEM((n1 * (n2 + FOURIER_ROW_PAD), FGROUP), F32),
                        pltpu.VMEM((n2 * (2 * n1 + FOURIER_ROW_PAD), FGROUP), F32)],
        compiler_params=pltpu.CompilerParams(
            dimension_semantics=("parallel", "parallel"), vmem_limit_bytes=V7X_VMEM_LIMIT),
        name="fourier_mix",
    )(u3, f1_b, f2_b, cs_b)


def _attn_kernel(q_ref, k_ref, v_ref, bias_ref, o_ref, *, rows, rq):
    rc = pl.program_id(2)
    lane = lax.broadcasted_iota(jnp.int32, (GRID_W, HEAD_PAIR), 1)
    first_head = lane < HEAD_DIM

    def band_keys(x):
        return [jnp.concatenate([x[a * GRID_W + t * BAND_W:a * GRID_W + (t + 1) * BAND_W]
                                 for a in range(WIN_ROWS)], axis=0) for t in range(N_BANDS)]

    def per_head(x):
        return x[:BAND_Q], x[BAND_Q:]

    def widen(x, t, fill):
        lo, hi = BAND_Q_RANGE[t]
        parts = []
        if lo:
            parts.append(jnp.full((lo,) + x.shape[1:], fill, x.dtype))
        parts.append(x)
        if hi < GRID_W:
            parts.append(jnp.full((GRID_W - hi,) + x.shape[1:], fill, x.dtype))
        return jnp.concatenate(parts, axis=0)

    def scores(i):
        r = rc * rq + i
        rs = jnp.clip(r - WIN_ROWS // 2, 0, rows - WIN_ROWS)
        k0 = pl.multiple_of(rs * GRID_W, GRID_W)
        q2 = q_ref[0, i * GRID_W:(i + 1) * GRID_W, :]
        zero = jnp.zeros_like(q2)
        qa = jnp.where(first_head, q2, zero)
        qb = jnp.where(first_head, zero, q2)
        kb = band_keys(k_ref[0, pl.ds(k0, KEY_ROWS), :])
        ss = []
        for t, (lo, hi) in enumerate(BAND_Q_RANGE):
            lhs = jnp.concatenate([qa[lo:hi], qb[lo:hi]], axis=0)
            s = lax.dot_general(lhs, kb[t], (((1,), (1,)), ((), ())), preferred_element_type=F32)
            ss.append(s + bias_ref[0, r - rs, t])
        return i, k0, ss

    def finish(item):
        i, k0, ss = item
        def row_stat(x):
            return jnp.broadcast_to(x, (x.shape[0], HEAD_PAIR))

        ms = [per_head(row_stat(jnp.max(s, axis=-1, keepdims=True))) for s in ss]
        m_all = [functools.reduce(jnp.maximum, [widen(ms[t][hh], t, NEG_INF) for t in range(N_BANDS)])
                 for hh in range(2)]
        ps, l_all = [], [0.0, 0.0]
        for t, (lo, hi) in enumerate(BAND_Q_RANGE):
            m_t = jnp.concatenate([m_all[0][lo:hi], m_all[1][lo:hi]], axis=0)
            m_t = jnp.concatenate([m_t] * (BAND_KEYS // HEAD_PAIR), axis=1)
            p = jnp.exp2(ss[t] - m_t)
            ls = per_head(row_stat(jnp.sum(p, axis=-1, keepdims=True)))
            for hh in range(2):
                l_all[hh] = l_all[hh] + widen(ls[hh], t, 0.0)
            ps.append(p.astype(BF16))
        vb = band_keys(v_ref[0, pl.ds(k0, KEY_ROWS), :])
        o_all = [0.0, 0.0]
        for t in range(N_BANDS):
            os_ = per_head(jnp.dot(ps[t], vb[t], preferred_element_type=F32))
            for hh in range(2):
                o_all[hh] = o_all[hh] + widen(os_[hh], t, 0.0)
        o2 = jnp.where(first_head, o_all[0] / l_all[0], o_all[1] / l_all[1])
        o_ref[0, i * GRID_W:(i + 1) * GRID_W, :] = o2.astype(BF16)

    pending = [scores(i) for i in range(min(ATT_AHEAD, rq))]
    for i in range(rq):
        if i + ATT_AHEAD < rq:
            pending.append(scores(i + ATT_AHEAD))
        finish(pending.pop(0))


def _attention(q3, k3, v3, bias):
    b, s, _ = q3.shape
    rows = s // GRID_W
    rq = ATT_ROWS
    assert s % GRID_W == 0 and rows % rq == 0 and rows >= WIN_ROWS, (s, rq)
    nhp = N_HEADS // 2
    kern = functools.partial(_attn_kernel, rows=rows, rq=rq)
    return pl.pallas_call(
        kern,
        grid=(b, nhp, rows // rq),
        in_specs=[pl.BlockSpec((1, rq * GRID_W, HEAD_PAIR), lambda bi, hp, rc: (bi, rc, hp)),
                  pl.BlockSpec((1, s, HEAD_PAIR), lambda bi, hp, rc: (bi, 0, hp)),
                  pl.BlockSpec((1, s, HEAD_PAIR), lambda bi, hp, rc: (bi, 0, hp)),
                  pl.BlockSpec((1, WIN_ROWS, N_BANDS, 2 * BAND_Q, BAND_KEYS),
                               lambda bi, hp, rc: (hp, 0, 0, 0, 0))],
        out_specs=pl.BlockSpec((1, rq * GRID_W, HEAD_PAIR), lambda bi, hp, rc: (bi, rc, hp)),
        out_shape=jax.ShapeDtypeStruct((b, s, ATT_WIDTH), BF16),
        compiler_params=pltpu.CompilerParams(
            dimension_semantics=("parallel", "parallel", "arbitrary"),
            vmem_limit_bytes=V7X_VMEM_LIMIT),
        name="nbr_attention",
    )(q3, k3, v3, bias)


def _post_kernel(f_ref, a_ref, x_ref, wf_ref, wa_ref, g1_ref, b1_ref, wg_ref, wu_ref, wd_ref,
                 g2_ref, b2_ref, o_ref, x1_ref, x1b_ref, acc_ref):
    subs = [slice(r, r + POST_SUB) for r in range(0, x_ref.shape[0], POST_SUB)]
    for rows in subs:
        mix = jnp.dot(f_ref[rows, :], wf_ref[...], preferred_element_type=F32)
        mix = mix + jnp.dot(a_ref[rows, :], wa_ref[...], preferred_element_type=F32)
        x1 = _layer_norm(ALPHA * x_ref[rows, :] + mix, g1_ref[...], b1_ref[...])
        x1_ref[rows, :] = x1
        x1b_ref[rows, :] = x1.astype(BF16)
    for rows in subs:
        xb = x1b_ref[rows, :]
        for c, lo in enumerate(range(0, D_FF, FF_CHUNK)):
            sl = slice(lo, lo + FF_CHUNK)
            gate = jnp.dot(xb, wg_ref[:, sl], preferred_element_type=F32)
            up = jnp.dot(xb, wu_ref[:, sl], preferred_element_type=F32)
            hid = (gate * jax.nn.sigmoid(gate) * up).astype(BF16)
            part = jnp.dot(hid, wd_ref[sl, :], preferred_element_type=F32)
            if c == 0:
                acc_ref[rows, :] = part
            else:
                acc_ref[rows, :] += part
        z = ALPHA * x1_ref[rows, :] + acc_ref[rows, :]
        o_ref[rows, :] = _layer_norm(z, g2_ref[...], b2_ref[...])


def _post(f2d, a2d, x2, p, layer):
    t = x2.shape[0]
    tm = TM_POST
    row = pl.BlockSpec((None, 1, D_MODEL), lambda i: (layer, 0, 0))
    resident = pl.Buffered(1)
    return pl.pallas_call(
        _post_kernel,
        grid=(t // tm,),
        in_specs=[pl.BlockSpec((tm, FOURIER_WIDTH), lambda i: (i, 0)),
                  pl.BlockSpec((tm, ATT_WIDTH), lambda i: (i, 0)),
                  pl.BlockSpec((tm, D_MODEL), lambda i: (i, 0)),
                  pl.BlockSpec((None, FOURIER_WIDTH, D_MODEL), lambda i: (layer, 0, 0),
                               pipeline_mode=resident),
                  pl.BlockSpec((None, ATT_WIDTH, D_MODEL), lambda i: (layer, 1, 0),
                               pipeline_mode=resident),
                  row, row,
                  pl.BlockSpec((None, D_MODEL, D_FF), lambda i: (layer, 0, 0), pipeline_mode=resident),
                  pl.BlockSpec((None, D_MODEL, D_FF), lambda i: (layer, 0, 0), pipeline_mode=resident),
                  pl.BlockSpec((None, D_FF, D_MODEL), lambda i: (layer, 0, 0), pipeline_mode=resident),
                  row, row],
        out_specs=pl.BlockSpec((tm, D_MODEL), lambda i: (i, 0)),
        out_shape=jax.ShapeDtypeStruct((t, D_MODEL), F32),
        scratch_shapes=[pltpu.VMEM((tm, D_MODEL), F32), pltpu.VMEM((tm, D_MODEL), BF16),
                        pltpu.VMEM((tm, D_MODEL), F32)],
        compiler_params=pltpu.CompilerParams(
            dimension_semantics=("parallel",), vmem_limit_bytes=V7X_VMEM_LIMIT),
        name="post_ffn",
    )(f2d, a2d, x2, p["w_out"], p["w_out"], p["ln1_g"], p["ln1_b"],
      p["w_gate"], p["w_up"], p["w_down"], p["ln2_g"], p["ln2_b"])


def _layer(x3, p, layer):
    b, s, d = x3.shape
    f1, f2, cs = _dft_constants(s)
    f1_b = jnp.asarray(f1).astype(BF16)
    f2_b = jnp.asarray(f2).astype(BF16)
    cs_b = jnp.asarray(cs).astype(BF16)

    x2d = x3.reshape(b * s, d)
    u, q, k, v = _in_proj(x2d, p["w_in"], layer)
    fmix = _fourier_mix(u.reshape(b, s, FOURIER_WIDTH), f1_b, f2_b, cs_b)
    att = _attention(q.reshape(b, s, ATT_WIDTH), k.reshape(b, s, ATT_WIDTH),
                     v.reshape(b, s, ATT_WIDTH), p["bias"])
    x2 = _post(fmix.reshape(b * s, FOURIER_WIDTH), att.reshape(b * s, ATT_WIDTH), x2d, p, layer)
    return x2.reshape(b, s, d)


def kernel(x_prompt, x_sample, w_in, w_out, rpb, ln1_g, ln1_b, w_gate, w_up, w_down, ln2_g, ln2_b):
    p = dict(
        w_in=w_in.astype(BF16), w_out=w_out.astype(BF16),
        w_gate=w_gate.astype(BF16), w_up=w_up.astype(BF16), w_down=w_down.astype(BF16),
        ln1_g=ln1_g.reshape(DEPTH, 1, D_MODEL), ln1_b=ln1_b.reshape(DEPTH, 1, D_MODEL),
        ln2_g=ln2_g.reshape(DEPTH, 1, D_MODEL), ln2_b=ln2_b.reshape(DEPTH, 1, D_MODEL),
    )
    y_prompt = x_prompt
    y_sample = x_sample
    for layer in range(DEPTH):
        p["bias"] = _bias_tables(rpb, layer)
        y_prompt = _layer(y_prompt, p, layer)
        y_sample = _layer(y_sample, p, layer)
    return (y_prompt, y_sample)
```

```python
import functools
import math

import jax
import jax.numpy as jnp
import numpy as np
from jax import lax
from jax.experimental import pallas as pl
from jax.experimental.pallas import tpu as pltpu

F32 = jnp.float32
BF16 = jnp.bfloat16

D_MODEL = 1024
DEPTH = 2
FOURIER_WIDTH = 512
FGROUP = 128
N_FGROUPS = FOURIER_WIDTH // FGROUP
ATT_WIDTH = 512
HEAD_DIM = 64
N_HEADS = ATT_WIDTH // HEAD_DIM
IN_WIDTH = FOURIER_WIDTH + 3 * ATT_WIDTH
GRID_W = 64
WIN_ROWS = 8
WIN_COLS = 16
D_FF = 2816
ALPHA = (2 * DEPTH) ** 0.25
LN_EPS = 1e-5
NEG_INF = -1e30
LOG2_E = math.log2(math.e)
ATT_SCALE = HEAD_DIM ** -0.5 * LOG2_E

DFT_N1 = 128
HEAD_PAIR = 2 * HEAD_DIM
KEY_ROWS = WIN_ROWS * GRID_W

V7X_VMEM_LIMIT = 56 * 1024 * 1024

TM_PROJ = 1024
TM_POST = 1024
POST_SUB = 512
FF_CHUNK = 256
FOURIER_ROW_PAD = 8
FOURIER_COPY_UNROLL = 64
FOURIER_UNROLL = 32
ATT_ROWS = 128
ATT_AHEAD = 3


def _layer_norm(z, g, b):
    mu = jnp.mean(z, axis=-1, keepdims=True)
    zc = z - mu
    var = jnp.mean(zc * zc, axis=-1, keepdims=True)
    return zc * lax.rsqrt(var + LN_EPS) * g + b


@functools.lru_cache(maxsize=None)
def _dft_constants(seq):
    n1 = DFT_N1
    n2 = seq // n1
    n = np.arange(n1)[None, None, :] * n2 + np.arange(n2)[:, None, None]
    m = (np.arange(n1)[None, :, None] * n) % seq
    th = 2.0 * np.pi * m / seq
    f1 = np.concatenate([np.cos(th), -np.sin(th)], axis=1) / math.sqrt(n1)
    kn = np.outer(np.arange(n2), np.arange(n2)) % n2
    th = 2.0 * np.pi * kn / n2
    c2, s2 = np.cos(th) / math.sqrt(n2), np.sin(th) / math.sqrt(n2)
    f2 = np.concatenate([np.concatenate([c2, s2], axis=1),
                         np.concatenate([-s2, c2], axis=1)], axis=0)
    cn = np.outer(np.arange(FGROUP), np.arange(FGROUP)) % FGROUP
    th = 2.0 * np.pi * cn / FGROUP
    cs = np.concatenate([np.cos(th), np.sin(th)], axis=0) / math.sqrt(FGROUP)
    return f1.astype(np.float32), f2.astype(np.float32), cs.astype(np.float32)


N_DR = 2 * WIN_ROWS - 1
N_DC = 2 * WIN_COLS - 1

BAND_W = 32
N_BANDS = GRID_W // BAND_W
BAND_KEYS = WIN_ROWS * BAND_W
QUERY_ALIGN = 16


def _band_query_ranges():
    c = np.arange(GRID_W)
    cs = np.clip(c - WIN_COLS // 2, 0, GRID_W - WIN_COLS)
    out = []
    for t in range(N_BANDS):
        sees = c[(cs < (t + 1) * BAND_W) & (cs + WIN_COLS > t * BAND_W)]
        lo = int(sees.min()) // QUERY_ALIGN * QUERY_ALIGN
        hi = -(-(int(sees.max()) + 1) // QUERY_ALIGN) * QUERY_ALIGN
        out.append((lo, hi))
    return out


BAND_Q_RANGE = _band_query_ranges()
BAND_Q = BAND_Q_RANGE[0][1] - BAND_Q_RANGE[0][0]
assert all(hi - lo == BAND_Q for lo, hi in BAND_Q_RANGE)


def _bias_kernel(rpb_ref, o_ref, *, layer):
    h = pl.program_id(0)
    c = lax.broadcasted_iota(jnp.int32, (GRID_W, HEAD_PAIR), 0)
    lane = lax.broadcasted_iota(jnp.int32, (GRID_W, HEAD_PAIR), 1)
    cs = jnp.clip(c - WIN_COLS // 2, 0, GRID_W - WIN_COLS)
    lane_w = lax.broadcasted_iota(jnp.int32, (8, HEAD_PAIR), 1) % GRID_W
    offsets = (0, BAND_W)
    valid = []
    for off in offsets:
        w = (lane - off) % GRID_W
        valid.append((w >= cs) & (w < cs + WIN_COLS))
    for dr in range(N_DR):
        base = ((layer * N_HEADS + h) * N_DR + dr) * N_DC
        vec = jnp.zeros((8, HEAD_PAIR), F32)
        for dc in range(N_DC):
            vec = jnp.where(lane_w == dc, rpb_ref[base + dc] * LOG2_E, vec)
        rep = jnp.concatenate([vec] * (GRID_W // 8), axis=0)
        tiles = []
        for off, ok in zip(offsets, valid):
            t_off = pltpu.roll(rep, (HEAD_PAIR - (WIN_COLS - 1) + off) % HEAD_PAIR, 1,
                               stride=1, stride_axis=0)
            tiles.append(jnp.where(ok, t_off, NEG_INF))
        for d in range(WIN_ROWS):
            a = dr - (WIN_ROWS - 1) + d
            if not 0 <= a < WIN_ROWS:
                continue
            dst = (a * BAND_W) % HEAD_PAIR
            for t, (lo, hi) in enumerate(BAND_Q_RANGE):
                tile = tiles[offsets.index((dst - t * BAND_W) % GRID_W)]
                o_ref[0, d, t, :, a * BAND_W:(a + 1) * BAND_W] = tile[lo:hi, dst:dst + BAND_W]


def _bias_tables(rpb, layer):
    return pl.pallas_call(
        functools.partial(_bias_kernel, layer=layer),
        grid=(N_HEADS,),
        in_specs=[pl.BlockSpec(memory_space=pltpu.SMEM)],
        out_specs=pl.BlockSpec((1, WIN_ROWS, N_BANDS, BAND_Q, BAND_KEYS),
                               lambda h: (h // 2, 0, 0, h % 2, 0)),
        out_shape=jax.ShapeDtypeStruct((N_HEADS // 2, WIN_ROWS, N_BANDS, 2 * BAND_Q, BAND_KEYS), F32),
        compiler_params=pltpu.CompilerParams(dimension_semantics=("parallel",)),
        name="bias_expand",
    )(rpb.astype(F32).reshape(-1))


def _in_proj_kernel(x_ref, w_ref, u_ref, q_ref, k_ref, v_ref):
    xb = x_ref[...].astype(BF16)
    lo = 0
    for o_ref, scale in ((u_ref, None), (q_ref, ATT_SCALE), (k_ref, None), (v_ref, None)):
        width = o_ref.shape[-1]
        h = jnp.dot(xb, w_ref[:, lo:lo + width], preferred_element_type=F32)
        if scale is not None:
            h = h * scale
        o_ref[...] = h.astype(BF16)
        lo += width


def _in_proj(x2, w_in_b, layer):
    t = x2.shape[0]
    tm = TM_PROJ
    widths = (FOURIER_WIDTH, ATT_WIDTH, ATT_WIDTH, ATT_WIDTH)
    return pl.pallas_call(
        _in_proj_kernel,
        grid=(t // tm,),
        in_specs=[pl.BlockSpec((tm, D_MODEL), lambda i: (i, 0)),
                  pl.BlockSpec((None, D_MODEL, IN_WIDTH), lambda i: (layer, 0, 0))],
        out_specs=[pl.BlockSpec((tm, wd), lambda i: (i, 0)) for wd in widths],
        out_shape=[jax.ShapeDtypeStruct((t, wd), BF16) for wd in widths],
        compiler_params=pltpu.CompilerParams(
            dimension_semantics=("parallel",), vmem_limit_bytes=V7X_VMEM_LIMIT),
        name="in_proj",
    )(x2, w_in_b)


def _fourier_kernel(u_ref, f1_ref, f2_ref, cs_ref, o_ref, tok_ref, y_ref, *, n1, n2):
    grp = FOURIER_UNROLL
    tp = n2 + FOURIER_ROW_PAD
    yp = 2 * n1 + FOURIER_ROW_PAD

    def stage_in(i, carry):
        src = pl.multiple_of(i * n2, n2)
        dst = pl.multiple_of(i * tp, 8)
        tok_ref[pl.ds(dst, n2), :] = u_ref[0, pl.ds(src, n2), :].astype(F32)
        return carry

    lax.fori_loop(0, n1, stage_in, 0, unroll=FOURIER_COPY_UNROLL)

    def stage1(nn, carry):
        xs = tok_ref[pl.ds(nn, n1, stride=tp), :].astype(BF16)
        z = jnp.dot(f1_ref[nn], xs, preferred_element_type=F32)
        y_ref[pl.ds(pl.multiple_of(nn * yp, 8), 2 * n1), :] = z
        return carry

    lax.fori_loop(0, n2, stage1, 0, unroll=FOURIER_COPY_UNROLL)

    def stage2(t, carry):
        zs = []
        for j in range(grp):
            k1 = t * grp + j
            zr = y_ref[pl.ds(k1, n2, stride=yp), :]
            zi = y_ref[pl.ds(n1 + k1, n2, stride=yp), :]
            zs.append(jnp.concatenate([zr, zi], axis=0).astype(BF16))
        xs = []
        for j in range(0, grp, 2):
            z2 = jnp.concatenate([zs[j], zs[j + 1]], axis=1)
            x2 = jnp.dot(f2_ref[...], z2, preferred_element_type=F32).astype(BF16)
            xs.extend([x2[:, :FGROUP], x2[:, FGROUP:]])
        for j, x in enumerate(xs):
            k1 = t * grp + j
            xc = jnp.concatenate([x[:n2], x[n2:]], axis=1)
            o = jnp.dot(xc, cs_ref[...], preferred_element_type=F32)
            tok_ref[pl.ds(pl.multiple_of(k1 * tp, 8), n2), :] = o
        return carry

    for t in range(n1 // grp):
        stage2(t, 0)

    def stage_out(k2, carry):
        o = tok_ref[pl.ds(k2, n1, stride=tp), :]
        o_ref[0, pl.ds(pl.multiple_of(k2 * n1, n1), n1), :] = o.astype(BF16)
        return carry

    lax.fori_loop(0, n2, stage_out, 0, unroll=FOURIER_COPY_UNROLL)


def _fourier_mix(u3, f1_b, f2_b, cs_b):
    b, s, _ = u3.shape
    n1 = DFT_N1
    n2 = s // n1
    kern = functools.partial(_fourier_kernel, n1=n1, n2=n2)
    return pl.pallas_call(
        kern,
        grid=(b, N_FGROUPS),
        in_specs=[pl.BlockSpec((1, s, FGROUP), lambda bi, g: (bi, 0, g)),
                  pl.BlockSpec((n2, 2 * n1, n1), lambda bi, g: (0, 0, 0),
                               pipeline_mode=pl.Buffered(1)),
                  pl.BlockSpec((2 * n2, 2 * n2), lambda bi, g: (0, 0)),
                  pl.BlockSpec((2 * FGROUP, FGROUP), lambda bi, g: (0, 0))],
        out_specs=pl.BlockSpec((1, s, FGROUP), lambda bi, g: (bi, 0, g)),
        out_shape=jax.ShapeDtypeStruct((b, s, FOURIER_WIDTH), BF16),
        scratch_shapes=[pltpu.VMEM((n1 * (n2 + FOURIER_ROW_PAD), FGROUP), F32),
                        pltpu.VMEM((n2 * (2 * n1 + FOURIER_ROW_PAD), FGROUP), F32)],
        compiler_params=pltpu.CompilerParams(
            dimension_semantics=("parallel", "parallel"), vmem_limit_bytes=V7X_VMEM_LIMIT),
        name="fourier_mix",
    )(u3, f1_b, f2_b, cs_b)


def _attn_kernel(q_ref, k_ref, v_ref, bias_ref, o_ref, *, rows, rq):
    rc = pl.program_id(2)
    lane = lax.broadcasted_iota(jnp.int32, (GRID_W, HEAD_PAIR), 1)
    first_head = lane < HEAD_DIM

    def band_keys(x):
        return [jnp.concatenate([x[a * GRID_W + t * BAND_W:a * GRID_W + (t + 1) * BAND_W]
                                 for a in range(WIN_ROWS)], axis=0) for t in range(N_BANDS)]

    def per_head(x):
        return x[:BAND_Q], x[BAND_Q:]

    def widen(x, t, fill):
        lo, hi = BAND_Q_RANGE[t]
        parts = []
        if lo:
            parts.append(jnp.full((lo,) + x.shape[1:], fill, x.dtype))
        parts.append(x)
        if hi < GRID_W:
            parts.append(jnp.full((GRID_W - hi,) + x.shape[1:], fill, x.dtype))
        return jnp.concatenate(parts, axis=0)

    def scores(i):
        r = rc * rq + i
        rs = jnp.clip(r - WIN_ROWS // 2, 0, rows - WIN_ROWS)
        k0 = pl.multiple_of(rs * GRID_W, GRID_W)
        q2 = q_ref[0, i * GRID_W:(i + 1) * GRID_W, :]
        zero = jnp.zeros_like(q2)
        qa = jnp.where(first_head, q2, zero)
        qb = jnp.where(first_head, zero, q2)
        kb = band_keys(k_ref[0, pl.ds(k0, KEY_ROWS), :])
        ss = []
        for t, (lo, hi) in enumerate(BAND_Q_RANGE):
            lhs = jnp.concatenate([qa[lo:hi], qb[lo:hi]], axis=0)
            s = lax.dot_general(lhs, kb[t], (((1,), (1,)), ((), ())), preferred_element_type=F32)
            ss.append(s + bias_ref[0, r - rs, t])
        return i, k0, ss

    def finish(item):
        i, k0, ss = item
        def row_stat(x):
            return jnp.broadcast_to(x, (x.shape[0], HEAD_PAIR))

        ms = [per_head(row_stat(jnp.max(s, axis=-1, keepdims=True))) for s in ss]
        m_all = [functools.reduce(jnp.maximum, [widen(ms[t][hh], t, NEG_INF) for t in range(N_BANDS)])
                 for hh in range(2)]
        ps, l_all = [], [0.0, 0.0]
        for t, (lo, hi) in enumerate(BAND_Q_RANGE):
            m_t = jnp.concatenate([m_all[0][lo:hi], m_all[1][lo:hi]], axis=0)
            m_t = jnp.concatenate([m_t] * (BAND_KEYS // HEAD_PAIR), axis=1)
            p = jnp.exp2(ss[t] - m_t)
            ls = per_head(row_stat(jnp.sum(p, axis=-1, keepdims=True)))
            for hh in range(2):
                l_all[hh] = l_all[hh] + widen(ls[hh], t, 0.0)
            ps.append(p.astype(BF16))
        vb = band_keys(v_ref[0, pl.ds(k0, KEY_ROWS), :])
        o_all = [0.0, 0.0]
        for t in range(N_BANDS):
            os_ = per_head(jnp.dot(ps[t], vb[t], preferred_element_type=F32))
            for hh in range(2):
                o_all[hh] = o_all[hh] + widen(os_[hh], t, 0.0)
        o2 = jnp.where(first_head, o_all[0] / l_all[0], o_all[1] / l_all[1])
        o_ref[0, i * GRID_W:(i + 1) * GRID_W, :] = o2.astype(BF16)

    pending = [scores(i) for i in range(min(ATT_AHEAD, rq))]
    for i in range(rq):
        if i + ATT_AHEAD < rq:
            pending.append(scores(i + ATT_AHEAD))
        finish(pending.pop(0))


def _attention(q3, k3, v3, bias):
    b, s, _ = q3.shape
    rows = s // GRID_W
    rq = ATT_ROWS
    assert s % GRID_W == 0 and rows % rq == 0 and rows >= WIN_ROWS, (s, rq)
    nhp = N_HEADS // 2
    kern = functools.partial(_attn_kernel, rows=rows, rq=rq)
    return pl.pallas_call(
        kern,
        grid=(b, nhp, rows // rq),
        in_specs=[pl.BlockSpec((1, rq * GRID_W, HEAD_PAIR), lambda bi, hp, rc: (bi, rc, hp)),
                  pl.BlockSpec((1, s, HEAD_PAIR), lambda bi, hp, rc: (bi, 0, hp)),
                  pl.BlockSpec((1, s, HEAD_PAIR), lambda bi, hp, rc: (bi, 0, hp)),
                  pl.BlockSpec((1, WIN_ROWS, N_BANDS, 2 * BAND_Q, BAND_KEYS),
                               lambda bi, hp, rc: (hp, 0, 0, 0, 0))],
        out_specs=pl.BlockSpec((1, rq * GRID_W, HEAD_PAIR), lambda bi, hp, rc: (bi, rc, hp)),
        out_shape=jax.ShapeDtypeStruct((b, s, ATT_WIDTH), BF16),
        compiler_params=pltpu.CompilerParams(
            dimension_semantics=("parallel", "parallel", "arbitrary"),
            vmem_limit_bytes=V7X_VMEM_LIMIT),
        name="nbr_attention",
    )(q3, k3, v3, bias)


def _post_kernel(f_ref, a_ref, x_ref, wf_ref, wa_ref, g1_ref, b1_ref, wg_ref, wu_ref, wd_ref,
                 g2_ref, b2_ref, o_ref, x1_ref, x1b_ref, acc_ref):
    subs = [slice(r, r + POST_SUB) for r in range(0, x_ref.shape[0], POST_SUB)]
    for rows in subs:
        mix = jnp.dot(f_ref[rows, :], wf_ref[...], preferred_element_type=F32)
        mix = mix + jnp.dot(a_ref[rows, :], wa_ref[...], preferred_element_type=F32)
        x1 = _layer_norm(ALPHA * x_ref[rows, :] + mix, g1_ref[...], b1_ref[...])
        x1_ref[rows, :] = x1
        x1b_ref[rows, :] = x1.astype(BF16)
    for rows in subs:
        xb = x1b_ref[rows, :]
        for c, lo in enumerate(range(0, D_FF, FF_CHUNK)):
            sl = slice(lo, lo + FF_CHUNK)
            gate = jnp.dot(xb, wg_ref[:, sl], preferred_element_type=F32)
            up = jnp.dot(xb, wu_ref[:, sl], preferred_element_type=F32)
            hid = (gate * jax.nn.sigmoid(gate) * up).astype(BF16)
            part = jnp.dot(hid, wd_ref[sl, :], preferred_element_type=F32)
            if c == 0:
                acc_ref[rows, :] = part
            else:
                acc_ref[rows, :] += part
        z = ALPHA * x1_ref[rows, :] + acc_ref[rows, :]
        o_ref[rows, :] = _layer_norm(z, g2_ref[...], b2_ref[...])


def _post(f2d, a2d, x2, p, layer):
    t = x2.shape[0]
    tm = TM_POST
    row = pl.BlockSpec((None, 1, D_MODEL), lambda i: (layer, 0, 0))
    resident = pl.Buffered(1)
    return pl.pallas_call(
        _post_kernel,
        grid=(t // tm,),
        in_specs=[pl.BlockSpec((tm, FOURIER_WIDTH), lambda i: (i, 0)),
                  pl.BlockSpec((tm, ATT_WIDTH), lambda i: (i, 0)),
                  pl.BlockSpec((tm, D_MODEL), lambda i: (i, 0)),
                  pl.BlockSpec((None, FOURIER_WIDTH, D_MODEL), lambda i: (layer, 0, 0),
                               pipeline_mode=resident),
                  pl.BlockSpec((None, ATT_WIDTH, D_MODEL), lambda i: (layer, 1, 0),
                               pipeline_mode=resident),
                  row, row,
                  pl.BlockSpec((None, D_MODEL, D_FF), lambda i: (layer, 0, 0), pipeline_mode=resident),
                  pl.BlockSpec((None, D_MODEL, D_FF), lambda i: (layer, 0, 0), pipeline_mode=resident),
                  pl.BlockSpec((None, D_FF, D_MODEL), lambda i: (layer, 0, 0), pipeline_mode=resident),
                  row, row],
        out_specs=pl.BlockSpec((tm, D_MODEL), lambda i: (i, 0)),
        out_shape=jax.ShapeDtypeStruct((t, D_MODEL), F32),
        scratch_shapes=[pltpu.VMEM((tm, D_MODEL), F32), pltpu.VMEM((tm, D_MODEL), BF16),
                        pltpu.VMEM((tm, D_MODEL), F32)],
        compiler_params=pltpu.CompilerParams(
            dimension_semantics=("parallel",), vmem_limit_bytes=V7X_VMEM_LIMIT),
        name="post_ffn",
    )(f2d, a2d, x2, p["w_out"], p["w_out"], p["ln1_g"], p["ln1_b"],
      p["w_gate"], p["w_up"], p["w_down"], p["ln2_g"], p["ln2_b"])


def _layer(x3, p, layer):
    b, s, d = x3.shape
    f1, f2, cs = _dft_constants(s)
    f1_b = jnp.asarray(f1).astype(BF16)
    f2_b = jnp.asarray(f2).astype(BF16)
    cs_b = jnp.asarray(cs).astype(BF16)

    x2d = x3.reshape(b * s, d)
    u, q, k, v = _in_proj(x2d, p["w_in"], layer)
    fmix = _fourier_mix(u.reshape(b, s, FOURIER_WIDTH), f1_b, f2_b, cs_b)
    att = _attention(q.reshape(b, s, ATT_WIDTH), k.reshape(b, s, ATT_WIDTH),
                     v.reshape(b, s, ATT_WIDTH), p["bias"])
    x2 = _post(fmix.reshape(b * s, FOURIER_WIDTH), att.reshape(b * s, ATT_WIDTH), x2d, p, layer)
    return x2.reshape(b, s, d)


def kernel(x_prompt, x_sample, w_in, w_out, rpb, ln1_g, ln1_b, w_gate, w_up, w_down, ln2_g, ln2_b):
    p = dict(
        w_in=w_in.astype(BF16), w_out=w_out.astype(BF16),
        w_gate=w_gate.astype(BF16), w_up=w_up.astype(BF16), w_down=w_down.astype(BF16),
        ln1_g=ln1_g.reshape(DEPTH, 1, D_MODEL), ln1_b=ln1_b.reshape(DEPTH, 1, D_MODEL),
        ln2_g=ln2_g.reshape(DEPTH, 1, D_MODEL), ln2_b=ln2_b.reshape(DEPTH, 1, D_MODEL),
    )
    y_prompt = x_prompt
    y_sample = x_sample
    for layer in range(DEPTH):
        p["bias"] = _bias_tables(rpb, layer)
        y_prompt = _layer(y_prompt, p, layer)
        y_sample = _layer(y_sample, p, layer)
    return (y_prompt, y_sample)
```

```python
import functools
import math

import jax
import jax.numpy as jnp
import numpy as np
from jax import lax
from jax.experimental import pallas as pl
from jax.experimental.pallas import tpu as pltpu

F32 = jnp.float32
BF16 = jnp.bfloat16

D_MODEL = 1024
DEPTH = 2
FOURIER_WIDTH = 512
FGROUP = 128
N_FGROUPS = FOURIER_WIDTH // FGROUP
ATT_WIDTH = 512
HEAD_DIM = 64
N_HEADS = ATT_WIDTH // HEAD_DIM
IN_WIDTH = FOURIER_WIDTH + 3 * ATT_WIDTH
GRID_W = 64
WIN_ROWS = 8
WIN_COLS = 16
D_FF = 2816
ALPHA = (2 * DEPTH) ** 0.25
LN_EPS = 1e-5
NEG_INF = -1e30
LOG2_E = math.log2(math.e)
ATT_SCALE = HEAD_DIM ** -0.5 * LOG2_E

DFT_N1 = 128
HEAD_PAIR = 2 * HEAD_DIM
KEY_ROWS = WIN_ROWS * GRID_W

V7X_VMEM_LIMIT = 56 * 1024 * 1024

TM_PROJ = 1024
TM_POST = 1024
POST_SUB = 512
FF_CHUNK = 256
FOURIER_ROW_PAD = 8
FOURIER_COPY_UNROLL = 64
FOURIER_UNROLL = 32
ATT_ROWS = 128
ATT_AHEAD = 3


def _layer_norm(z, g, b):
    mu = jnp.mean(z, axis=-1, keepdims=True)
    zc = z - mu
    var = jnp.mean(zc * zc, axis=-1, keepdims=True)
    return zc * lax.rsqrt(var + LN_EPS) * g + b


@functools.lru_cache(maxsize=None)
def _dft_constants(seq):
    n1 = DFT_N1
    n2 = seq // n1
    n = np.arange(n1)[None, None, :] * n2 + np.arange(n2)[:, None, None]
    m = (np.arange(n1)[None, :, None] * n) % seq
    th = 2.0 * np.pi * m / seq
    f1 = np.concatenate([np.cos(th), -np.sin(th)], axis=1) / math.sqrt(n1)
    kn = np.outer(np.arange(n2), np.arange(n2)) % n2
    th = 2.0 * np.pi * kn / n2
    c2, s2 = np.cos(th) / math.sqrt(n2), np.sin(th) / math.sqrt(n2)
    f2 = np.concatenate([np.concatenate([c2, s2], axis=1),
                         np.concatenate([-s2, c2], axis=1)], axis=0)
    cn = np.outer(np.arange(FGROUP), np.arange(FGROUP)) % FGROUP
    th = 2.0 * np.pi * cn / FGROUP
    cs = np.concatenate([np.cos(th), np.sin(th)], axis=0) / math.sqrt(FGROUP)
    return f1.astype(np.float32), f2.astype(np.float32), cs.astype(np.float32)


N_DR = 2 * WIN_ROWS - 1
N_DC = 2 * WIN_COLS - 1

BAND_W = 32
N_BANDS = GRID_W // BAND_W
BAND_KEYS = WIN_ROWS * BAND_W
QUERY_ALIGN = 16


def _band_query_ranges():
    c = np.arange(GRID_W)
    cs = np.clip(c - WIN_COLS // 2, 0, GRID_W - WIN_COLS)
    out = []
    for t in range(N_BANDS):
        sees = c[(cs < (t + 1) * BAND_W) & (cs + WIN_COLS > t * BAND_W)]
        lo = int(sees.min()) // QUERY_ALIGN * QUERY_ALIGN
        hi = -(-(int(sees.max()) + 1) // QUERY_ALIGN) * QUERY_ALIGN
        out.append((lo, hi))
    return out


BAND_Q_RANGE = _band_query_ranges()
BAND_Q = BAND_Q_RANGE[0][1] - BAND_Q_RANGE[0][0]
assert all(hi - lo == BAND_Q for lo, hi in BAND_Q_RANGE)


def _bias_kernel(rpb_ref, o_ref, *, layer):
    h = pl.program_id(0)
    c = lax.broadcasted_iota(jnp.int32, (GRID_W, HEAD_PAIR), 0)
    lane = lax.broadcasted_iota(jnp.int32, (GRID_W, HEAD_PAIR), 1)
    cs = jnp.clip(c - WIN_COLS // 2, 0, GRID_W - WIN_COLS)
    lane_w = lax.broadcasted_iota(jnp.int32, (8, HEAD_PAIR), 1) % GRID_W
    offsets = (0, BAND_W)
    valid = []
    for off in offsets:
        w = (lane - off) % GRID_W
        valid.append((w >= cs) & (w < cs + WIN_COLS))
    for dr in range(N_DR):
        base = ((layer * N_HEADS + h) * N_DR + dr) * N_DC
        vec = jnp.zeros((8, HEAD_PAIR), F32)
        for dc in range(N_DC):
            vec = jnp.where(lane_w == dc, rpb_ref[base + dc] * LOG2_E, vec)
        rep = jnp.concatenate([vec] * (GRID_W // 8), axis=0)
        tiles = []
        for off, ok in zip(offsets, valid):
            t_off = pltpu.roll(rep, (HEAD_PAIR - (WIN_COLS - 1) + off) % HEAD_PAIR, 1,
                               stride=1, stride_axis=0)
            tiles.append(jnp.where(ok, t_off, NEG_INF))
        for d in range(WIN_ROWS):
            a = dr - (WIN_ROWS - 1) + d
            if not 0 <= a < WIN_ROWS:
                continue
            dst = (a * BAND_W) % HEAD_PAIR
            for t, (lo, hi) in enumerate(BAND_Q_RANGE):
                tile = tiles[offsets.index((dst - t * BAND_W) % GRID_W)]
                o_ref[0, d, t, :, a * BAND_W:(a + 1) * BAND_W] = tile[lo:hi, dst:dst + BAND_W]


def _bias_tables(rpb, layer):
    return pl.pallas_call(
        functools.partial(_bias_kernel, layer=layer),
        grid=(N_HEADS,),
        in_specs=[pl.BlockSpec(memory_space=pltpu.SMEM)],
        out_specs=pl.BlockSpec((1, WIN_ROWS, N_BANDS, BAND_Q, BAND_KEYS),
                               lambda h: (h // 2, 0, 0, h % 2, 0)),
        out_shape=jax.ShapeDtypeStruct((N_HEADS // 2, WIN_ROWS, N_BANDS, 2 * BAND_Q, BAND_KEYS), F32),
        compiler_params=pltpu.CompilerParams(dimension_semantics=("parallel",)),
        name="bias_expand",
    )(rpb.astype(F32).reshape(-1))


def _in_proj_kernel(x_ref, w_ref, u_ref, q_ref, k_ref, v_ref):
    xb = x_ref[...].astype(BF16)
    lo = 0
    for o_ref, scale in ((u_ref, None), (q_ref, ATT_SCALE), (k_ref, None), (v_ref, None)):
        width = o_ref.shape[-1]
        h = jnp.dot(xb, w_ref[:, lo:lo + width], preferred_element_type=F32)
        if scale is not None:
            h = h * scale
        o_ref[...] = h.astype(BF16)
        lo += width


def _in_proj(x2, w_in_b, layer):
    t = x2.shape[0]
    tm = TM_PROJ
    assert t % tm == 0, (t, tm)
    widths = (FOURIER_WIDTH, ATT_WIDTH, ATT_WIDTH, ATT_WIDTH)
    return pl.pallas_call(
        _in_proj_kernel,
        grid=(t // tm,),
        in_specs=[pl.BlockSpec((tm, D_MODEL), lambda i: (i, 0)),
                  pl.BlockSpec((None, D_MODEL, IN_WIDTH), lambda i: (layer, 0, 0))],
        out_specs=[pl.BlockSpec((tm, wd), lambda i: (i, 0)) for wd in widths],
        out_shape=[jax.ShapeDtypeStruct((t, wd), BF16) for wd in widths],
        compiler_params=pltpu.CompilerParams(
            dimension_semantics=("parallel",), vmem_limit_bytes=V7X_VMEM_LIMIT),
        name="in_proj",
    )(x2, w_in_b)


def _fourier_kernel(u_ref, f1_ref, f2_ref, cs_ref, *rest, n1, n2, n_cast):
    w_in_refs = rest[:n_cast]
    o_ref = rest[n_cast]
    w_out_refs = rest[n_cast + 1:2 * n_cast + 1]
    tok_ref, y_ref = rest[2 * n_cast + 1:]
    for w_src, w_dst in zip(w_in_refs, w_out_refs):
        w_dst[...] = w_src[...].astype(BF16)
    grp = FOURIER_UNROLL
    tp = n2 + FOURIER_ROW_PAD
    yp = 2 * n1 + FOURIER_ROW_PAD

    def stage_in(i, carry):
        src = pl.multiple_of(i * n2, n2)
        dst = pl.multiple_of(i * tp, 8)
        tok_ref[pl.ds(dst, n2), :] = u_ref[0, pl.ds(src, n2), :].astype(F32)
        return carry

    lax.fori_loop(0, n1, stage_in, 0, unroll=FOURIER_COPY_UNROLL)

    def stage1(nn, carry):
        xs = tok_ref[pl.ds(nn, n1, stride=tp), :].astype(BF16)
        z = jnp.dot(f1_ref[nn], xs, preferred_element_type=F32)
        y_ref[pl.ds(pl.multiple_of(nn * yp, 8), 2 * n1), :] = z
        return carry

    lax.fori_loop(0, n2, stage1, 0, unroll=FOURIER_COPY_UNROLL)

    def stage2(t, carry):
        zs = []
        for j in range(grp):
            k1 = t * grp + j
            zr = y_ref[pl.ds(k1, n2, stride=yp), :]
            zi = y_ref[pl.ds(n1 + k1, n2, stride=yp), :]
            zs.append(jnp.concatenate([zr, zi], axis=0).astype(BF16))
        xs = []
        for j in range(0, grp, 2):
            z2 = jnp.concatenate([zs[j], zs[j + 1]], axis=1)
            x2 = jnp.dot(f2_ref[...], z2, preferred_element_type=F32).astype(BF16)
            xs.extend([x2[:, :FGROUP], x2[:, FGROUP:]])
        for j, x in enumerate(xs):
            k1 = t * grp + j
            xc = jnp.concatenate([x[:n2], x[n2:]], axis=1)
            o = jnp.dot(xc, cs_ref[...], preferred_element_type=F32)
            tok_ref[pl.ds(pl.multiple_of(k1 * tp, 8), n2), :] = o
        return carry

    for t in range(n1 // grp):
        stage2(t, 0)

    def stage_out(k2, carry):
        o = tok_ref[pl.ds(k2, n1, stride=tp), :]
        o_ref[0, pl.ds(pl.multiple_of(k2 * n1, n1), n1), :] = o.astype(BF16)
        return carry

    lax.fori_loop(0, n2, stage_out, 0, unroll=FOURIER_COPY_UNROLL)


def _fourier_mix(u3, f1_b, f2_b, cs_b, cast_weights=(), layer=0):
    b, s, _ = u3.shape
    n1 = DFT_N1
    n2 = s // n1
    assert s == n1 * n2 and n2 % QUERY_ALIGN == 0 and n1 % FOURIER_UNROLL == 0, (s, n1, n2)
    n_steps = b * N_FGROUPS
    step = lambda bi, g: bi * N_FGROUPS + g
    cast_in, cast_out, cast_shapes = [], [], []
    for w in cast_weights:
        _, rows, cols = w.shape
        slab = rows // n_steps
        assert rows % n_steps == 0 and slab % QUERY_ALIGN == 0, (w.shape, n_steps)
        cast_in.append(pl.BlockSpec((None, slab, cols), lambda bi, g: (layer, step(bi, g), 0)))
        cast_out.append(pl.BlockSpec((slab, cols), lambda bi, g: (step(bi, g), 0)))
        cast_shapes.append(jax.ShapeDtypeStruct((rows, cols), BF16))
    kern = functools.partial(_fourier_kernel, n1=n1, n2=n2, n_cast=len(cast_weights))
    outs = pl.pallas_call(
        kern,
        grid=(b, N_FGROUPS),
        in_specs=[pl.BlockSpec((1, s, FGROUP), lambda bi, g: (bi, 0, g)),
                  pl.BlockSpec((n2, 2 * n1, n1), lambda bi, g: (0, 0, 0),
                               pipeline_mode=pl.Buffered(1)),
                  pl.BlockSpec((2 * n2, 2 * n2), lambda bi, g: (0, 0)),
                  pl.BlockSpec((2 * FGROUP, FGROUP), lambda bi, g: (0, 0))] + cast_in,
        out_specs=[pl.BlockSpec((1, s, FGROUP), lambda bi, g: (bi, 0, g))] + cast_out,
        out_shape=[jax.ShapeDtypeStruct((b, s, FOURIER_WIDTH), BF16)] + cast_shapes,
        scratch_shapes=[pltpu.VMEM((n1 * (n2 + FOURIER_ROW_PAD), FGROUP), F32),
                        pltpu.VMEM((n2 * (2 * n1 + FOURIER_ROW_PAD), FGROUP), F32)],
        compiler_params=pltpu.CompilerParams(
            dimension_semantics=("parallel", "parallel"), vmem_limit_bytes=V7X_VMEM_LIMIT),
        name="fourier_mix",
    )(u3, f1_b, f2_b, cs_b, *cast_weights)
    return outs[0], tuple(outs[1:])


def _attn_kernel(q_ref, k_ref, v_ref, bias_ref, o_ref, *, rows, rq):
    rc = pl.program_id(2)
    lane = lax.broadcasted_iota(jnp.int32, (GRID_W, HEAD_PAIR), 1)
    first_head = lane < HEAD_DIM

    def band_keys(x):
        return [jnp.concatenate([x[a * GRID_W + t * BAND_W:a * GRID_W + (t + 1) * BAND_W]
                                 for a in range(WIN_ROWS)], axis=0) for t in range(N_BANDS)]

    def per_head(x):
        return x[:BAND_Q], x[BAND_Q:]

    def widen(x, t, fill):
        lo, hi = BAND_Q_RANGE[t]
        parts = []
        if lo:
            parts.append(jnp.full((lo,) + x.shape[1:], fill, x.dtype))
        parts.append(x)
        if hi < GRID_W:
            parts.append(jnp.full((GRID_W - hi,) + x.shape[1:], fill, x.dtype))
        return jnp.concatenate(parts, axis=0)

    def scores(i):
        r = rc * rq + i
        rs = jnp.clip(r - WIN_ROWS // 2, 0, rows - WIN_ROWS)
        k0 = pl.multiple_of(rs * GRID_W, GRID_W)
        q2 = q_ref[0, i * GRID_W:(i + 1) * GRID_W, :]
        zero = jnp.zeros_like(q2)
        qa = jnp.where(first_head, q2, zero)
        qb = jnp.where(first_head, zero, q2)
        kb = band_keys(k_ref[0, pl.ds(k0, KEY_ROWS), :])
        ss = []
        for t, (lo, hi) in enumerate(BAND_Q_RANGE):
            lhs = jnp.concatenate([qa[lo:hi], qb[lo:hi]], axis=0)
            s = lax.dot_general(lhs, kb[t], (((1,), (1,)), ((), ())), preferred_element_type=F32)
            ss.append(s + bias_ref[0, r - rs, t])
        return i, k0, ss

    def finish(item):
        i, k0, ss = item
        def row_stat(x):
            return jnp.broadcast_to(x, (x.shape[0], HEAD_PAIR))

        ms = [per_head(row_stat(jnp.max(s, axis=-1, keepdims=True))) for s in ss]
        m_all = [functools.reduce(jnp.maximum, [widen(ms[t][hh], t, NEG_INF) for t in range(N_BANDS)])
                 for hh in range(2)]
        ps, l_all = [], [0.0, 0.0]
        for t, (lo, hi) in enumerate(BAND_Q_RANGE):
            m_t = jnp.concatenate([m_all[0][lo:hi], m_all[1][lo:hi]], axis=0)
            m_t = jnp.concatenate([m_t] * (BAND_KEYS // HEAD_PAIR), axis=1)
            p = jnp.exp2(ss[t] - m_t)
            ls = per_head(row_stat(jnp.sum(p, axis=-1, keepdims=True)))
            for hh in range(2):
                l_all[hh] = l_all[hh] + widen(ls[hh], t, 0.0)
            ps.append(p.astype(BF16))
        vb = band_keys(v_ref[0, pl.ds(k0, KEY_ROWS), :])
        o_all = [0.0, 0.0]
        for t in range(N_BANDS):
            os_ = per_head(jnp.dot(ps[t], vb[t], preferred_element_type=F32))
            for hh in range(2):
                o_all[hh] = o_all[hh] + widen(os_[hh], t, 0.0)
        o2 = jnp.where(first_head, o_all[0] / l_all[0], o_all[1] / l_all[1])
        o_ref[0, i * GRID_W:(i + 1) * GRID_W, :] = o2.astype(BF16)

    pending = [scores(i) for i in range(min(ATT_AHEAD, rq))]
    for i in range(rq):
        if i + ATT_AHEAD < rq:
            pending.append(scores(i + ATT_AHEAD))
        finish(pending.pop(0))


def _attention(q3, k3, v3, bias):
    b, s, _ = q3.shape
    rows = s // GRID_W
    rq = ATT_ROWS
    assert s % GRID_W == 0 and rows % rq == 0 and rows >= WIN_ROWS, (s, rq)
    nhp = N_HEADS // 2
    kern = functools.partial(_attn_kernel, rows=rows, rq=rq)
    return pl.pallas_call(
        kern,
        grid=(b, nhp, rows // rq),
        in_specs=[pl.BlockSpec((1, rq * GRID_W, HEAD_PAIR), lambda bi, hp, rc: (bi, rc, hp)),
                  pl.BlockSpec((1, s, HEAD_PAIR), lambda bi, hp, rc: (bi, 0, hp)),
                  pl.BlockSpec((1, s, HEAD_PAIR), lambda bi, hp, rc: (bi, 0, hp)),
                  pl.BlockSpec((1, WIN_ROWS, N_BANDS, 2 * BAND_Q, BAND_KEYS),
                               lambda bi, hp, rc: (hp, 0, 0, 0, 0))],
        out_specs=pl.BlockSpec((1, rq * GRID_W, HEAD_PAIR), lambda bi, hp, rc: (bi, rc, hp)),
        out_shape=jax.ShapeDtypeStruct((b, s, ATT_WIDTH), BF16),
        compiler_params=pltpu.CompilerParams(
            dimension_semantics=("parallel", "parallel", "arbitrary"),
            vmem_limit_bytes=V7X_VMEM_LIMIT),
        name="nbr_attention",
    )(q3, k3, v3, bias)


def _post_kernel(f_ref, a_ref, x_ref, wf_ref, wa_ref, g1_ref, b1_ref, wg_ref, wu_ref, wd_ref,
                 g2_ref, b2_ref, o_ref, x1_ref, x1b_ref, acc_ref):
    subs = [slice(r, r + POST_SUB) for r in range(0, x_ref.shape[0], POST_SUB)]
    for rows in subs:
        mix = jnp.dot(f_ref[rows, :], wf_ref[...], preferred_element_type=F32)
        mix = mix + jnp.dot(a_ref[rows, :], wa_ref[...], preferred_element_type=F32)
        x1 = _layer_norm(ALPHA * x_ref[rows, :] + mix, g1_ref[...], b1_ref[...])
        x1_ref[rows, :] = x1
        x1b_ref[rows, :] = x1.astype(BF16)
    for rows in subs:
        xb = x1b_ref[rows, :]
        for c, lo in enumerate(range(0, D_FF, FF_CHUNK)):
            sl = slice(lo, lo + FF_CHUNK)
            gate = jnp.dot(xb, wg_ref[:, sl], preferred_element_type=F32)
            up = jnp.dot(xb, wu_ref[:, sl], preferred_element_type=F32)
            hid = (gate * jax.nn.sigmoid(gate) * up).astype(BF16)
            part = jnp.dot(hid, wd_ref[sl, :], preferred_element_type=F32)
            if c == 0:
                acc_ref[rows, :] = part
            else:
                acc_ref[rows, :] += part
        z = ALPHA * x1_ref[rows, :] + acc_ref[rows, :]
        o_ref[rows, :] = _layer_norm(z, g2_ref[...], b2_ref[...])


def _post(f2d, a2d, x2, p, w_b, layer):
    t = x2.shape[0]
    tm = TM_POST
    assert t % tm == 0 and tm % POST_SUB == 0 and D_FF % FF_CHUNK == 0, (t, tm)
    row = pl.BlockSpec((None, 1, D_MODEL), lambda i: (layer, 0, 0))
    resident = pl.Buffered(1)
    return pl.pallas_call(
        _post_kernel,
        grid=(t // tm,),
        in_specs=[pl.BlockSpec((tm, FOURIER_WIDTH), lambda i: (i, 0)),
                  pl.BlockSpec((tm, ATT_WIDTH), lambda i: (i, 0)),
                  pl.BlockSpec((tm, D_MODEL), lambda i: (i, 0)),
                  pl.BlockSpec((FOURIER_WIDTH, D_MODEL), lambda i: (0, 0), pipeline_mode=resident),
                  pl.BlockSpec((ATT_WIDTH, D_MODEL), lambda i: (1, 0), pipeline_mode=resident),
                  row, row,
                  pl.BlockSpec((D_MODEL, D_FF), lambda i: (0, 0), pipeline_mode=resident),
                  pl.BlockSpec((D_MODEL, D_FF), lambda i: (0, 0), pipeline_mode=resident),
                  pl.BlockSpec((D_FF, D_MODEL), lambda i: (0, 0), pipeline_mode=resident),
                  row, row],
        out_specs=pl.BlockSpec((tm, D_MODEL), lambda i: (i, 0)),
        out_shape=jax.ShapeDtypeStruct((t, D_MODEL), F32),
        scratch_shapes=[pltpu.VMEM((tm, D_MODEL), F32), pltpu.VMEM((tm, D_MODEL), BF16),
                        pltpu.VMEM((tm, D_MODEL), F32)],
        compiler_params=pltpu.CompilerParams(
            dimension_semantics=("parallel",), vmem_limit_bytes=V7X_VMEM_LIMIT),
        name="post_ffn",
    )(f2d, a2d, x2, w_b["w_out"], w_b["w_out"], p["ln1_g"], p["ln1_b"],
      w_b["w_gate"], w_b["w_up"], w_b["w_down"], p["ln2_g"], p["ln2_b"])


CAST_IN_FOURIER = ("w_out", "w_gate", "w_up", "w_down")


def _layer(x3, p, w_b, layer):
    b, s, d = x3.shape
    f1, f2, cs = _dft_constants(s)
    f1_b = jnp.asarray(f1).astype(BF16)
    f2_b = jnp.asarray(f2).astype(BF16)
    cs_b = jnp.asarray(cs).astype(BF16)

    x2d = x3.reshape(b * s, d)
    u, q, k, v = _in_proj(x2d, p["w_in"], layer)
    to_cast = () if w_b is not None else tuple(p[n] for n in CAST_IN_FOURIER)
    fmix, cast = _fourier_mix(u.reshape(b, s, FOURIER_WIDTH), f1_b, f2_b, cs_b, to_cast, layer)
    if w_b is None:
        w_b = dict(zip(CAST_IN_FOURIER, cast))
    att = _attention(q.reshape(b, s, ATT_WIDTH), k.reshape(b, s, ATT_WIDTH),
                     v.reshape(b, s, ATT_WIDTH), p["bias"])
    x2 = _post(fmix.reshape(b * s, FOURIER_WIDTH), att.reshape(b * s, ATT_WIDTH), x2d, p, w_b, layer)
    return x2.reshape(b, s, d), w_b


def kernel(x_prompt, x_sample, w_in, w_out, rpb, ln1_g, ln1_b, w_gate, w_up, w_down, ln2_g, ln2_b):
    p = dict(
        w_in=w_in.astype(BF16), w_out=w_out, w_gate=w_gate, w_up=w_up, w_down=w_down,
        ln1_g=ln1_g.reshape(DEPTH, 1, D_MODEL), ln1_b=ln1_b.reshape(DEPTH, 1, D_MODEL),
        ln2_g=ln2_g.reshape(DEPTH, 1, D_MODEL), ln2_b=ln2_b.reshape(DEPTH, 1, D_MODEL),
    )
    y_prompt = x_prompt
    y_sample = x_sample
    for layer in range(DEPTH):
        p["bias"] = _bias_tables(rpb, layer)
        y_prompt, w_b = _layer(y_prompt, p, None, layer)
        y_sample, _ = _layer(y_sample, p, w_b, layer)
    return (y_prompt, y_sample)
```

```python
import functools
import math

import jax
import jax.numpy as jnp
import numpy as np
from jax import lax
from jax.experimental import pallas as pl
from jax.experimental.pallas import tpu as pltpu

F32 = jnp.float32
BF16 = jnp.bfloat16

D_MODEL = 1024
DEPTH = 2
FOURIER_WIDTH = 512
FGROUP = 128
N_FGROUPS = FOURIER_WIDTH // FGROUP
ATT_WIDTH = 512
HEAD_DIM = 64
N_HEADS = ATT_WIDTH // HEAD_DIM
IN_WIDTH = FOURIER_WIDTH + 3 * ATT_WIDTH
GRID_W = 64
WIN_ROWS = 8
WIN_COLS = 16
D_FF = 2816
ALPHA = (2 * DEPTH) ** 0.25
LN_EPS = 1e-5
NEG_INF = -1e30
LOG2_E = math.log2(math.e)
ATT_SCALE = HEAD_DIM ** -0.5 * LOG2_E

DFT_N1 = 128
HEAD_PAIR = 2 * HEAD_DIM
KEY_ROWS = WIN_ROWS * GRID_W

V7X_VMEM_LIMIT = 56 * 1024 * 1024

TM_PROJ = 2048
TM_POST = 1024
POST_SUB = 512
FF_CHUNK = 256
FOURIER_ROW_PAD = 8
FOURIER_COPY_UNROLL = 64
FOURIER_UNROLL = 32
ATT_ROWS = 128
ATT_AHEAD = 3


def _layer_norm(z, g, b):
    mu = jnp.mean(z, axis=-1, keepdims=True)
    zc = z - mu
    var = jnp.mean(zc * zc, axis=-1, keepdims=True)
    return zc * lax.rsqrt(var + LN_EPS) * g + b


@functools.lru_cache(maxsize=None)
def _dft_constants(seq):
    n1 = DFT_N1
    n2 = seq // n1
    n = np.arange(n1)[None, None, :] * n2 + np.arange(n2)[:, None, None]
    m = (np.arange(n1)[None, :, None] * n) % seq
    th = 2.0 * np.pi * m / seq
    f1 = np.concatenate([np.cos(th), -np.sin(th)], axis=1) / math.sqrt(n1)
    kn = np.outer(np.arange(n2), np.arange(n2)) % n2
    th = 2.0 * np.pi * kn / n2
    c2, s2 = np.cos(th) / math.sqrt(n2), np.sin(th) / math.sqrt(n2)
    f2 = np.concatenate([np.concatenate([c2, s2], axis=1),
                         np.concatenate([-s2, c2], axis=1)], axis=0)
    cn = np.outer(np.arange(FGROUP), np.arange(FGROUP)) % FGROUP
    th = 2.0 * np.pi * cn / FGROUP
    cs = np.concatenate([np.cos(th), np.sin(th)], axis=0) / math.sqrt(FGROUP)
    return f1.astype(np.float32), f2.astype(np.float32), cs.astype(np.float32)


N_DR = 2 * WIN_ROWS - 1
N_DC = 2 * WIN_COLS - 1

BAND_W = 32
N_BANDS = GRID_W // BAND_W
BAND_KEYS = WIN_ROWS * BAND_W
QUERY_ALIGN = 16


def _band_query_ranges():
    c = np.arange(GRID_W)
    cs = np.clip(c - WIN_COLS // 2, 0, GRID_W - WIN_COLS)
    out = []
    for t in range(N_BANDS):
        sees = c[(cs < (t + 1) * BAND_W) & (cs + WIN_COLS > t * BAND_W)]
        lo = int(sees.min()) // QUERY_ALIGN * QUERY_ALIGN
        hi = -(-(int(sees.max()) + 1) // QUERY_ALIGN) * QUERY_ALIGN
        out.append((lo, hi))
    return out


BAND_Q_RANGE = _band_query_ranges()
BAND_Q = BAND_Q_RANGE[0][1] - BAND_Q_RANGE[0][0]
assert all(hi - lo == BAND_Q for lo, hi in BAND_Q_RANGE)


def _bias_kernel(rpb_ref, o_ref, *, layer):
    h = pl.program_id(0)
    c = lax.broadcasted_iota(jnp.int32, (GRID_W, HEAD_PAIR), 0)
    lane = lax.broadcasted_iota(jnp.int32, (GRID_W, HEAD_PAIR), 1)
    cs = jnp.clip(c - WIN_COLS // 2, 0, GRID_W - WIN_COLS)
    lane_w = lax.broadcasted_iota(jnp.int32, (8, HEAD_PAIR), 1) % GRID_W
    offsets = (0, BAND_W)
    valid = []
    for off in offsets:
        w = (lane - off) % GRID_W
        valid.append((w >= cs) & (w < cs + WIN_COLS))
    for dr in range(N_DR):
        base = ((layer * N_HEADS + h) * N_DR + dr) * N_DC
        vec = jnp.zeros((8, HEAD_PAIR), F32)
        for dc in range(N_DC):
            vec = jnp.where(lane_w == dc, rpb_ref[base + dc] * LOG2_E, vec)
        rep = jnp.concatenate([vec] * (GRID_W // 8), axis=0)
        tiles = []
        for off, ok in zip(offsets, valid):
            t_off = pltpu.roll(rep, (HEAD_PAIR - (WIN_COLS - 1) + off) % HEAD_PAIR, 1,
                               stride=1, stride_axis=0)
            tiles.append(jnp.where(ok, t_off, NEG_INF))
        for d in range(WIN_ROWS):
            a = dr - (WIN_ROWS - 1) + d
            if not 0 <= a < WIN_ROWS:
                continue
            dst = (a * BAND_W) % HEAD_PAIR
            for t, (lo, hi) in enumerate(BAND_Q_RANGE):
                tile = tiles[offsets.index((dst - t * BAND_W) % GRID_W)]
                o_ref[0, d, t, :, a * BAND_W:(a + 1) * BAND_W] = tile[lo:hi, dst:dst + BAND_W]


def _bias_tables(rpb, layer):
    return pl.pallas_call(
        functools.partial(_bias_kernel, layer=layer),
        grid=(N_HEADS,),
        in_specs=[pl.BlockSpec(memory_space=pltpu.SMEM)],
        out_specs=pl.BlockSpec((1, WIN_ROWS, N_BANDS, BAND_Q, BAND_KEYS),
                               lambda h: (h // 2, 0, 0, h % 2, 0)),
        out_shape=jax.ShapeDtypeStruct((N_HEADS // 2, WIN_ROWS, N_BANDS, 2 * BAND_Q, BAND_KEYS), F32),
        compiler_params=pltpu.CompilerParams(dimension_semantics=("parallel",)),
        name="bias_expand",
    )(rpb.astype(F32).reshape(-1))


def _in_proj_kernel(x_ref, w_ref, u_ref, q_ref, k_ref, v_ref):
    xb = x_ref[...].astype(BF16)
    lo = 0
    for o_ref, scale in ((u_ref, None), (q_ref, ATT_SCALE), (k_ref, None), (v_ref, None)):
        width = o_ref.shape[-1]
        h = jnp.dot(xb, w_ref[:, lo:lo + width], preferred_element_type=F32)
        if scale is not None:
            h = h * scale
        o_ref[...] = h.astype(BF16)
        lo += width


def _in_proj(x2, w_in_b, layer):
    t = x2.shape[0]
    tm = TM_PROJ
    assert t % tm == 0, (t, tm)
    widths = (FOURIER_WIDTH, ATT_WIDTH, ATT_WIDTH, ATT_WIDTH)
    return pl.pallas_call(
        _in_proj_kernel,
        grid=(t // tm,),
        in_specs=[pl.BlockSpec((tm, D_MODEL), lambda i: (i, 0)),
                  pl.BlockSpec((None, D_MODEL, IN_WIDTH), lambda i: (layer, 0, 0))],
        out_specs=[pl.BlockSpec((tm, wd), lambda i: (i, 0)) for wd in widths],
        out_shape=[jax.ShapeDtypeStruct((t, wd), BF16) for wd in widths],
        compiler_params=pltpu.CompilerParams(
            dimension_semantics=("parallel",), vmem_limit_bytes=V7X_VMEM_LIMIT),
        name="in_proj",
    )(x2, w_in_b)


def _fourier_kernel(u_ref, f1_ref, f2_ref, cs_ref, *rest, n1, n2, n_cast):
    w_in_refs = rest[:n_cast]
    o_ref = rest[n_cast]
    w_out_refs = rest[n_cast + 1:2 * n_cast + 1]
    tok_ref, y_ref = rest[2 * n_cast + 1:]
    for w_src, w_dst in zip(w_in_refs, w_out_refs):
        w_dst[...] = w_src[...].astype(BF16)
    grp = FOURIER_UNROLL
    tp = n2 + FOURIER_ROW_PAD
    yp = 2 * n1 + FOURIER_ROW_PAD

    def stage_in(i, carry):
        src = pl.multiple_of(i * n2, n2)
        dst = pl.multiple_of(i * tp, 8)
        tok_ref[pl.ds(dst, n2), :] = u_ref[0, pl.ds(src, n2), :].astype(F32)
        return carry

    lax.fori_loop(0, n1, stage_in, 0, unroll=FOURIER_COPY_UNROLL)

    def stage1(nn, carry):
        xs = tok_ref[pl.ds(nn, n1, stride=tp), :].astype(BF16)
        z = jnp.dot(f1_ref[nn], xs, preferred_element_type=F32)
        y_ref[pl.ds(pl.multiple_of(nn * yp, 8), 2 * n1), :] = z
        return carry

    lax.fori_loop(0, n2, stage1, 0, unroll=FOURIER_COPY_UNROLL)

    def stage2(t, carry):
        zs = []
        for j in range(grp):
            k1 = t * grp + j
            zr = y_ref[pl.ds(k1, n2, stride=yp), :]
            zi = y_ref[pl.ds(n1 + k1, n2, stride=yp), :]
            zs.append(jnp.concatenate([zr, zi], axis=0).astype(BF16))
        xs = []
        for j in range(0, grp, 2):
            z2 = jnp.concatenate([zs[j], zs[j + 1]], axis=1)
            x2 = jnp.dot(f2_ref[...], z2, preferred_element_type=F32).astype(BF16)
            xs.extend([x2[:, :FGROUP], x2[:, FGROUP:]])
        for j, x in enumerate(xs):
            k1 = t * grp + j
            xc = jnp.concatenate([x[:n2], x[n2:]], axis=1)
            o = jnp.dot(xc, cs_ref[...], preferred_element_type=F32)
            tok_ref[pl.ds(pl.multiple_of(k1 * tp, 8), n2), :] = o
        return carry

    for t in range(n1 // grp):
        stage2(t, 0)

    def stage_out(k2, carry):
        o = tok_ref[pl.ds(k2, n1, stride=tp), :]
        o_ref[0, pl.ds(pl.multiple_of(k2 * n1, n1), n1), :] = o.astype(BF16)
        return carry

    lax.fori_loop(0, n2, stage_out, 0, unroll=FOURIER_COPY_UNROLL)


def _fourier_mix(u3, f1_b, f2_b, cs_b, cast_weights=(), layer=0):
    b, s, _ = u3.shape
    n1 = DFT_N1
    n2 = s // n1
    assert s == n1 * n2 and n2 % QUERY_ALIGN == 0 and n1 % FOURIER_UNROLL == 0, (s, n1, n2)
    n_steps = b * N_FGROUPS
    step = lambda bi, g: bi * N_FGROUPS + g
    cast_in, cast_out, cast_shapes = [], [], []
    for w in cast_weights:
        _, rows, cols = w.shape
        slab = rows // n_steps
        assert rows % n_steps == 0 and slab % QUERY_ALIGN == 0, (w.shape, n_steps)
        cast_in.append(pl.BlockSpec((None, slab, cols), lambda bi, g: (layer, step(bi, g), 0)))
        cast_out.append(pl.BlockSpec((slab, cols), lambda bi, g: (step(bi, g), 0)))
        cast_shapes.append(jax.ShapeDtypeStruct((rows, cols), BF16))
    kern = functools.partial(_fourier_kernel, n1=n1, n2=n2, n_cast=len(cast_weights))
    outs = pl.pallas_call(
        kern,
        grid=(b, N_FGROUPS),
        in_specs=[pl.BlockSpec((1, s, FGROUP), lambda bi, g: (bi, 0, g)),
                  pl.BlockSpec((n2, 2 * n1, n1), lambda bi, g: (0, 0, 0),
                               pipeline_mode=pl.Buffered(1)),
                  pl.BlockSpec((2 * n2, 2 * n2), lambda bi, g: (0, 0)),
                  pl.BlockSpec((2 * FGROUP, FGROUP), lambda bi, g: (0, 0))] + cast_in,
        out_specs=[pl.BlockSpec((1, s, FGROUP), lambda bi, g: (bi, 0, g))] + cast_out,
        out_shape=[jax.ShapeDtypeStruct((b, s, FOURIER_WIDTH), BF16)] + cast_shapes,
        scratch_shapes=[pltpu.VMEM((n1 * (n2 + FOURIER_ROW_PAD), FGROUP), F32),
                        pltpu.VMEM((n2 * (2 * n1 + FOURIER_ROW_PAD), FGROUP), F32)],
        compiler_params=pltpu.CompilerParams(
            dimension_semantics=("parallel", "parallel"), vmem_limit_bytes=V7X_VMEM_LIMIT),
        name="fourier_mix",
    )(u3, f1_b, f2_b, cs_b, *cast_weights)
    return outs[0], tuple(outs[1:])


def _attn_kernel(q_ref, k_ref, v_ref, bias_ref, o_ref, *, rows, rq):
    rc = pl.program_id(2)
    lane = lax.broadcasted_iota(jnp.int32, (GRID_W, HEAD_PAIR), 1)
    first_head = lane < HEAD_DIM

    def band_keys(x):
        return [jnp.concatenate([x[a * GRID_W + t * BAND_W:a * GRID_W + (t + 1) * BAND_W]
                                 for a in range(WIN_ROWS)], axis=0) for t in range(N_BANDS)]

    def per_head(x):
        return x[:BAND_Q], x[BAND_Q:]

    def widen(x, t, fill):
        lo, hi = BAND_Q_RANGE[t]
        parts = []
        if lo:
            parts.append(jnp.full((lo,) + x.shape[1:], fill, x.dtype))
        parts.append(x)
        if hi < GRID_W:
            parts.append(jnp.full((GRID_W - hi,) + x.shape[1:], fill, x.dtype))
        return jnp.concatenate(parts, axis=0)

    def scores(i):
        r = rc * rq + i
        rs = jnp.clip(r - WIN_ROWS // 2, 0, rows - WIN_ROWS)
        k0 = pl.multiple_of(rs * GRID_W, GRID_W)
        q2 = q_ref[0, i * GRID_W:(i + 1) * GRID_W, :]
        zero = jnp.zeros_like(q2)
        qa = jnp.where(first_head, q2, zero)
        qb = jnp.where(first_head, zero, q2)
        kb = band_keys(k_ref[0, pl.ds(k0, KEY_ROWS), :])
        ss = []
        for t, (lo, hi) in enumerate(BAND_Q_RANGE):
            lhs = jnp.concatenate([qa[lo:hi], qb[lo:hi]], axis=0)
            s = lax.dot_general(lhs, kb[t], (((1,), (1,)), ((), ())), preferred_element_type=F32)
            ss.append(s + bias_ref[0, r - rs, t])
        return i, k0, ss

    def finish(item):
        i, k0, ss = item
        def row_stat(x):
            return jnp.broadcast_to(x, (x.shape[0], HEAD_PAIR))

        ms = [per_head(row_stat(jnp.max(s, axis=-1, keepdims=True))) for s in ss]
        m_all = [functools.reduce(jnp.maximum, [widen(ms[t][hh], t, NEG_INF) for t in range(N_BANDS)])
                 for hh in range(2)]
        ps, l_all = [], [0.0, 0.0]
        for t, (lo, hi) in enumerate(BAND_Q_RANGE):
            m_t = jnp.concatenate([m_all[0][lo:hi], m_all[1][lo:hi]], axis=0)
            m_t = jnp.concatenate([m_t] * (BAND_KEYS // HEAD_PAIR), axis=1)
            p = jnp.exp2(ss[t] - m_t)
            ls = per_head(row_stat(jnp.sum(p, axis=-1, keepdims=True)))
            for hh in range(2):
                l_all[hh] = l_all[hh] + widen(ls[hh], t, 0.0)
            ps.append(p.astype(BF16))
        vb = band_keys(v_ref[0, pl.ds(k0, KEY_ROWS), :])
        o_all = [0.0, 0.0]
        for t in range(N_BANDS):
            os_ = per_head(jnp.dot(ps[t], vb[t], preferred_element_type=F32))
            for hh in range(2):
                o_all[hh] = o_all[hh] + widen(os_[hh], t, 0.0)
        o2 = jnp.where(first_head, o_all[0] / l_all[0], o_all[1] / l_all[1])
        o_ref[0, i * GRID_W:(i + 1) * GRID_W, :] = o2.astype(BF16)

    pending = [scores(i) for i in range(min(ATT_AHEAD, rq))]
    for i in range(rq):
        if i + ATT_AHEAD < rq:
            pending.append(scores(i + ATT_AHEAD))
        finish(pending.pop(0))


def _attention(q3, k3, v3, bias):
    b, s, _ = q3.shape
    rows = s // GRID_W
    rq = ATT_ROWS
    assert s % GRID_W == 0 and rows % rq == 0 and rows >= WIN_ROWS, (s, rq)
    nhp = N_HEADS // 2
    kern = functools.partial(_attn_kernel, rows=rows, rq=rq)
    return pl.pallas_call(
        kern,
        grid=(b, nhp, rows // rq),
        in_specs=[pl.BlockSpec((1, rq * GRID_W, HEAD_PAIR), lambda bi, hp, rc: (bi, rc, hp)),
                  pl.BlockSpec((1, s, HEAD_PAIR), lambda bi, hp, rc: (bi, 0, hp)),
                  pl.BlockSpec((1, s, HEAD_PAIR), lambda bi, hp, rc: (bi, 0, hp)),
                  pl.BlockSpec((1, WIN_ROWS, N_BANDS, 2 * BAND_Q, BAND_KEYS),
                               lambda bi, hp, rc: (hp, 0, 0, 0, 0))],
        out_specs=pl.BlockSpec((1, rq * GRID_W, HEAD_PAIR), lambda bi, hp, rc: (bi, rc, hp)),
        out_shape=jax.ShapeDtypeStruct((b, s, ATT_WIDTH), BF16),
        compiler_params=pltpu.CompilerParams(
            dimension_semantics=("parallel", "parallel", "arbitrary"),
            vmem_limit_bytes=V7X_VMEM_LIMIT),
        name="nbr_attention",
    )(q3, k3, v3, bias)


def _post_kernel(f_ref, a_ref, x_ref, wf_ref, wa_ref, g1_ref, b1_ref, wg_ref, wu_ref, wd_ref,
                 g2_ref, b2_ref, o_ref, x1_ref, x1b_ref, acc_ref):
    subs = [slice(r, r + POST_SUB) for r in range(0, x_ref.shape[0], POST_SUB)]
    for rows in subs:
        mix = jnp.dot(f_ref[rows, :], wf_ref[...], preferred_element_type=F32)
        mix = mix + jnp.dot(a_ref[rows, :], wa_ref[...], preferred_element_type=F32)
        x1 = _layer_norm(ALPHA * x_ref[rows, :] + mix, g1_ref[...], b1_ref[...])
        x1_ref[rows, :] = x1
        x1b_ref[rows, :] = x1.astype(BF16)
    for rows in subs:
        xb = x1b_ref[rows, :]
        for c, lo in enumerate(range(0, D_FF, FF_CHUNK)):
            sl = slice(lo, lo + FF_CHUNK)
            gate = jnp.dot(xb, wg_ref[:, sl], preferred_element_type=F32)
            up = jnp.dot(xb, wu_ref[:, sl], preferred_element_type=F32)
            hid = (gate * jax.nn.sigmoid(gate) * up).astype(BF16)
            part = jnp.dot(hid, wd_ref[sl, :], preferred_element_type=F32)
            if c == 0:
                acc_ref[rows, :] = part
            else:
                acc_ref[rows, :] += part
        z = ALPHA * x1_ref[rows, :] + acc_ref[rows, :]
        o_ref[rows, :] = _layer_norm(z, g2_ref[...], b2_ref[...])


def _post(f2d, a2d, x2, p, w_b, layer):
    t = x2.shape[0]
    tm = TM_POST
    assert t % tm == 0 and tm % POST_SUB == 0 and D_FF % FF_CHUNK == 0, (t, tm)
    row = pl.BlockSpec((None, 1, D_MODEL), lambda i: (layer, 0, 0))
    resident = pl.Buffered(1)
    return pl.pallas_call(
        _post_kernel,
        grid=(t // tm,),
        in_specs=[pl.BlockSpec((tm, FOURIER_WIDTH), lambda i: (i, 0)),
                  pl.BlockSpec((tm, ATT_WIDTH), lambda i: (i, 0)),
                  pl.BlockSpec((tm, D_MODEL), lambda i: (i, 0)),
                  pl.BlockSpec((FOURIER_WIDTH, D_MODEL), lambda i: (0, 0), pipeline_mode=resident),
                  pl.BlockSpec((ATT_WIDTH, D_MODEL), lambda i: (1, 0), pipeline_mode=resident),
                  row, row,
                  pl.BlockSpec((D_MODEL, D_FF), lambda i: (0, 0), pipeline_mode=resident),
                  pl.BlockSpec((D_MODEL, D_FF), lambda i: (0, 0), pipeline_mode=resident),
                  pl.BlockSpec((D_FF, D_MODEL), lambda i: (0, 0), pipeline_mode=resident),
                  row, row],
        out_specs=pl.BlockSpec((tm, D_MODEL), lambda i: (i, 0)),
        out_shape=jax.ShapeDtypeStruct((t, D_MODEL), F32),
        scratch_shapes=[pltpu.VMEM((tm, D_MODEL), F32), pltpu.VMEM((tm, D_MODEL), BF16),
                        pltpu.VMEM((tm, D_MODEL), F32)],
        compiler_params=pltpu.CompilerParams(
            dimension_semantics=("parallel",), vmem_limit_bytes=V7X_VMEM_LIMIT),
        name="post_ffn",
    )(f2d, a2d, x2, w_b["w_out"], w_b["w_out"], p["ln1_g"], p["ln1_b"],
      w_b["w_gate"], w_b["w_up"], w_b["w_down"], p["ln2_g"], p["ln2_b"])


CAST_IN_FOURIER = ("w_out", "w_gate", "w_up", "w_down")


def _layer(x3, p, w_b, layer):
    b, s, d = x3.shape
    f1, f2, cs = _dft_constants(s)
    f1_b = jnp.asarray(f1).astype(BF16)
    f2_b = jnp.asarray(f2).astype(BF16)
    cs_b = jnp.asarray(cs).astype(BF16)

    x2d = x3.reshape(b * s, d)
    u, q, k, v = _in_proj(x2d, p["w_in"], layer)
    to_cast = () if w_b is not None else tuple(p[n] for n in CAST_IN_FOURIER)
    fmix, cast = _fourier_mix(u.reshape(b, s, FOURIER_WIDTH), f1_b, f2_b, cs_b, to_cast, layer)
    if w_b is None:
        w_b = dict(zip(CAST_IN_FOURIER, cast))
    att = _attention(q.reshape(b, s, ATT_WIDTH), k.reshape(b, s, ATT_WIDTH),
                     v.reshape(b, s, ATT_WIDTH), p["bias"])
    x2 = _post(fmix.reshape(b * s, FOURIER_WIDTH), att.reshape(b * s, ATT_WIDTH), x2d, p, w_b, layer)
    return x2.reshape(b, s, d), w_b


def kernel(x_prompt, x_sample, w_in, w_out, rpb, ln1_g, ln1_b, w_gate, w_up, w_down, ln2_g, ln2_b):
    p = dict(
        w_in=w_in.astype(BF16), w_out=w_out, w_gate=w_gate, w_up=w_up, w_down=w_down,
        ln1_g=ln1_g.reshape(DEPTH, 1, D_MODEL), ln1_b=ln1_b.reshape(DEPTH, 1, D_MODEL),
        ln2_g=ln2_g.reshape(DEPTH, 1, D_MODEL), ln2_b=ln2_b.reshape(DEPTH, 1, D_MODEL),
    )
    y_prompt = x_prompt
    y_sample = x_sample
    for layer in range(DEPTH):
        p["bias"] = _bias_tables(rpb, layer)
        y_prompt, w_b = _layer(y_prompt, p, None, layer)
        y_sample, _ = _layer(y_sample, p, w_b, layer)
    return (y_prompt, y_sample)
```

```python
import functools
import math

import jax
import jax.numpy as jnp
import numpy as np
from jax import lax
from jax.experimental import pallas as pl
from jax.experimental.pallas import tpu as pltpu

F32 = jnp.float32
BF16 = jnp.bfloat16

D_MODEL = 1024
DEPTH = 2
FOURIER_WIDTH = 512
FGROUP = 128
N_FGROUPS = FOURIER_WIDTH // FGROUP
ATT_WIDTH = 512
HEAD_DIM = 64
N_HEADS = ATT_WIDTH // HEAD_DIM
IN_WIDTH = FOURIER_WIDTH + 3 * ATT_WIDTH
GRID_W = 64
WIN_ROWS = 8
WIN_COLS = 16
D_FF = 2816
ALPHA = (2 * DEPTH) ** 0.25
LN_EPS = 1e-5
NEG_INF = -1e30
LOG2_E = math.log2(math.e)
ATT_SCALE = HEAD_DIM ** -0.5 * LOG2_E

DFT_N1 = 128
HEAD_PAIR = 2 * HEAD_DIM
KEY_ROWS = WIN_ROWS * GRID_W

V7X_VMEM_LIMIT = 56 * 1024 * 1024

TM_PROJ = 1024
TM_POST = 1024
POST_SUB = 512
FF_CHUNK = 256
FOURIER_ROW_PAD = 8
FOURIER_COPY_UNROLL = 64
FOURIER_UNROLL = 32
ATT_ROWS = 128
ATT_AHEAD = 3


def _layer_norm(z, g, b):
    mu = jnp.mean(z, axis=-1, keepdims=True)
    zc = z - mu
    var = jnp.mean(zc * zc, axis=-1, keepdims=True)
    return zc * lax.rsqrt(var + LN_EPS) * g + b


@functools.lru_cache(maxsize=None)
def _dft_constants(seq):
    n1 = DFT_N1
    n2 = seq // n1
    n = np.arange(n1)[None, None, :] * n2 + np.arange(n2)[:, None, None]
    m = (np.arange(n1)[None, :, None] * n) % seq
    th = 2.0 * np.pi * m / seq
    f1 = np.concatenate([np.cos(th), -np.sin(th)], axis=1) / math.sqrt(n1)
    kn = np.outer(np.arange(n2), np.arange(n2)) % n2
    th = 2.0 * np.pi * kn / n2
    c2, s2 = np.cos(th) / math.sqrt(n2), np.sin(th) / math.sqrt(n2)
    f2 = np.concatenate([np.concatenate([c2, s2], axis=1),
                         np.concatenate([-s2, c2], axis=1)], axis=0)
    cn = np.outer(np.arange(FGROUP), np.arange(FGROUP)) % FGROUP
    th = 2.0 * np.pi * cn / FGROUP
    cs = np.concatenate([np.cos(th), np.sin(th)], axis=0) / math.sqrt(FGROUP)
    return f1.astype(np.float32), f2.astype(np.float32), cs.astype(np.float32)


N_DR = 2 * WIN_ROWS - 1
N_DC = 2 * WIN_COLS - 1

BAND_W = 32
N_BANDS = GRID_W // BAND_W
BAND_KEYS = WIN_ROWS * BAND_W
QUERY_ALIGN = 16


def _band_query_ranges():
    c = np.arange(GRID_W)
    cs = np.clip(c - WIN_COLS // 2, 0, GRID_W - WIN_COLS)
    out = []
    for t in range(N_BANDS):
        sees = c[(cs < (t + 1) * BAND_W) & (cs + WIN_COLS > t * BAND_W)]
        lo = int(sees.min()) // QUERY_ALIGN * QUERY_ALIGN
        hi = -(-(int(sees.max()) + 1) // QUERY_ALIGN) * QUERY_ALIGN
        out.append((lo, hi))
    return out


BAND_Q_RANGE = _band_query_ranges()
BAND_Q = BAND_Q_RANGE[0][1] - BAND_Q_RANGE[0][0]
assert all(hi - lo == BAND_Q for lo, hi in BAND_Q_RANGE)


def _bias_kernel(rpb_ref, o_ref, *, layer):
    h = pl.program_id(0)
    c = lax.broadcasted_iota(jnp.int32, (GRID_W, HEAD_PAIR), 0)
    lane = lax.broadcasted_iota(jnp.int32, (GRID_W, HEAD_PAIR), 1)
    cs = jnp.clip(c - WIN_COLS // 2, 0, GRID_W - WIN_COLS)
    lane_w = lax.broadcasted_iota(jnp.int32, (8, HEAD_PAIR), 1) % GRID_W
    offsets = (0, BAND_W)
    valid = []
    for off in offsets:
        w = (lane - off) % GRID_W
        valid.append((w >= cs) & (w < cs + WIN_COLS))
    for dr in range(N_DR):
        base = ((layer * N_HEADS + h) * N_DR + dr) * N_DC
        vec = jnp.zeros((8, HEAD_PAIR), F32)
        for dc in range(N_DC):
            vec = jnp.where(lane_w == dc, rpb_ref[base + dc] * LOG2_E, vec)
        rep = jnp.concatenate([vec] * (GRID_W // 8), axis=0)
        tiles = []
        for off, ok in zip(offsets, valid):
            t_off = pltpu.roll(rep, (HEAD_PAIR - (WIN_COLS - 1) + off) % HEAD_PAIR, 1,
                               stride=1, stride_axis=0)
            tiles.append(jnp.where(ok, t_off, NEG_INF))
        for d in range(WIN_ROWS):
            a = dr - (WIN_ROWS - 1) + d
            if not 0 <= a < WIN_ROWS:
                continue
            dst = (a * BAND_W) % HEAD_PAIR
            for t, (lo, hi) in enumerate(BAND_Q_RANGE):
                tile = tiles[offsets.index((dst - t * BAND_W) % GRID_W)]
                o_ref[0, d, t, :, a * BAND_W:(a + 1) * BAND_W] = tile[lo:hi, dst:dst + BAND_W]


def _bias_tables(rpb, layer):
    return pl.pallas_call(
        functools.partial(_bias_kernel, layer=layer),
        grid=(N_HEADS,),
        in_specs=[pl.BlockSpec(memory_space=pltpu.SMEM)],
        out_specs=pl.BlockSpec((1, WIN_ROWS, N_BANDS, BAND_Q, BAND_KEYS),
                               lambda h: (h // 2, 0, 0, h % 2, 0)),
        out_shape=jax.ShapeDtypeStruct((N_HEADS // 2, WIN_ROWS, N_BANDS, 2 * BAND_Q, BAND_KEYS), F32),
        compiler_params=pltpu.CompilerParams(dimension_semantics=("parallel",)),
        name="bias_expand",
    )(rpb.astype(F32).reshape(-1))


def _in_proj_kernel(x_ref, w_ref, u_ref, q_ref, k_ref, v_ref):
    xb = x_ref[...].astype(BF16)
    lo = 0
    for o_ref, scale in ((u_ref, None), (q_ref, ATT_SCALE), (k_ref, None), (v_ref, None)):
        width = o_ref.shape[-1]
        h = jnp.dot(xb, w_ref[:, lo:lo + width], preferred_element_type=F32)
        if scale is not None:
            h = h * scale
        o_ref[...] = h.astype(BF16)
        lo += width


def _in_proj(x2, w_in_b, layer):
    t = x2.shape[0]
    tm = TM_PROJ
    assert t % tm == 0, (t, tm)
    widths = (FOURIER_WIDTH, ATT_WIDTH, ATT_WIDTH, ATT_WIDTH)
    return pl.pallas_call(
        _in_proj_kernel,
        grid=(t // tm,),
        in_specs=[pl.BlockSpec((tm, D_MODEL), lambda i: (i, 0)),
                  pl.BlockSpec((None, D_MODEL, IN_WIDTH), lambda i: (layer, 0, 0))],
        out_specs=[pl.BlockSpec((tm, wd), lambda i: (i, 0)) for wd in widths],
        out_shape=[jax.ShapeDtypeStruct((t, wd), BF16) for wd in widths],
        compiler_params=pltpu.CompilerParams(
            dimension_semantics=("parallel",), vmem_limit_bytes=V7X_VMEM_LIMIT),
        name="in_proj",
    )(x2, w_in_b)


def _fourier_kernel(u_ref, f1_ref, f2_ref, cs_ref, *rest, n1, n2, n_cast):
    w_in_refs = rest[:n_cast]
    o_ref = rest[n_cast]
    w_out_refs = rest[n_cast + 1:2 * n_cast + 1]
    tok_ref, y_ref = rest[2 * n_cast + 1:]
    for w_src, w_dst in zip(w_in_refs, w_out_refs):
        w_dst[...] = w_src[...].astype(BF16)
    grp = FOURIER_UNROLL
    tp = n2 + FOURIER_ROW_PAD
    yp = 2 * n1 + FOURIER_ROW_PAD

    def stage_in(i, carry):
        src = pl.multiple_of(i * n2, n2)
        dst = pl.multiple_of(i * tp, 8)
        tok_ref[pl.ds(dst, n2), :] = u_ref[0, pl.ds(src, n2), :].astype(F32)
        return carry

    lax.fori_loop(0, n1, stage_in, 0, unroll=FOURIER_COPY_UNROLL)

    def stage1(nn, carry):
        xs = tok_ref[pl.ds(nn, n1, stride=tp), :].astype(BF16)
        z = jnp.dot(f1_ref[nn], xs, preferred_element_type=F32)
        y_ref[pl.ds(pl.multiple_of(nn * yp, 8), 2 * n1), :] = z
        return carry

    lax.fori_loop(0, n2, stage1, 0, unroll=FOURIER_COPY_UNROLL)

    def stage2(t, carry):
        zs = []
        for j in range(grp):
            k1 = t * grp + j
            zr = y_ref[pl.ds(k1, n2, stride=yp), :]
            zi = y_ref[pl.ds(n1 + k1, n2, stride=yp), :]
            zs.append(jnp.concatenate([zr, zi], axis=0).astype(BF16))
        xs = []
        for j in range(0, grp, 2):
            z2 = jnp.concatenate([zs[j], zs[j + 1]], axis=1)
            x2 = jnp.dot(f2_ref[...], z2, preferred_element_type=F32).astype(BF16)
            xs.extend([x2[:, :FGROUP], x2[:, FGROUP:]])
        for j, x in enumerate(xs):
            k1 = t * grp + j
            xc = jnp.concatenate([x[:n2], x[n2:]], axis=1)
            o = jnp.dot(xc, cs_ref[...], preferred_element_type=F32)
            tok_ref[pl.ds(pl.multiple_of(k1 * tp, 8), n2), :] = o
        return carry

    for t in range(n1 // grp):
        stage2(t, 0)

    def stage_out(k2, carry):
        o = tok_ref[pl.ds(k2, n1, stride=tp), :]
        o_ref[0, pl.ds(pl.multiple_of(k2 * n1, n1), n1), :] = o.astype(BF16)
        return carry

    lax.fori_loop(0, n2, stage_out, 0, unroll=FOURIER_COPY_UNROLL)


def _fourier_mix(u3, f1_b, f2_b, cs_b, cast_weights=(), layer=0):
    b, s, _ = u3.shape
    n1 = DFT_N1
    n2 = s // n1
    assert s == n1 * n2 and n2 % QUERY_ALIGN == 0 and n1 % FOURIER_UNROLL == 0, (s, n1, n2)
    n_steps = b * N_FGROUPS
    step = lambda bi, g: bi * N_FGROUPS + g
    cast_in, cast_out, cast_shapes = [], [], []
    for w in cast_weights:
        _, rows, cols = w.shape
        slab = rows // n_steps
        assert rows % n_steps == 0 and slab % QUERY_ALIGN == 0, (w.shape, n_steps)
        cast_in.append(pl.BlockSpec((None, slab, cols), lambda bi, g: (layer, step(bi, g), 0)))
        cast_out.append(pl.BlockSpec((slab, cols), lambda bi, g: (step(bi, g), 0)))
        cast_shapes.append(jax.ShapeDtypeStruct((rows, cols), BF16))
    kern = functools.partial(_fourier_kernel, n1=n1, n2=n2, n_cast=len(cast_weights))
    outs = pl.pallas_call(
        kern,
        grid=(b, N_FGROUPS),
        in_specs=[pl.BlockSpec((1, s, FGROUP), lambda bi, g: (bi, 0, g)),
                  pl.BlockSpec((n2, 2 * n1, n1), lambda bi, g: (0, 0, 0),
                               pipeline_mode=pl.Buffered(1)),
                  pl.BlockSpec((2 * n2, 2 * n2), lambda bi, g: (0, 0)),
                  pl.BlockSpec((2 * FGROUP, FGROUP), lambda bi, g: (0, 0))] + cast_in,
        out_specs=[pl.BlockSpec((1, s, FGROUP), lambda bi, g: (bi, 0, g))] + cast_out,
        out_shape=[jax.ShapeDtypeStruct((b, s, FOURIER_WIDTH), BF16)] + cast_shapes,
        scratch_shapes=[pltpu.VMEM((n1 * (n2 + FOURIER_ROW_PAD), FGROUP), F32),
                        pltpu.VMEM((n2 * (2 * n1 + FOURIER_ROW_PAD), FGROUP), F32)],
        compiler_params=pltpu.CompilerParams(
            dimension_semantics=("parallel", "parallel"), vmem_limit_bytes=V7X_VMEM_LIMIT),
        name="fourier_mix",
    )(u3, f1_b, f2_b, cs_b, *cast_weights)
    return outs[0], tuple(outs[1:])


def _attn_kernel(q_ref, k_ref, v_ref, bias_ref, o_ref, *, rows, rq):
    rc = pl.program_id(2)
    lane = lax.broadcasted_iota(jnp.int32, (GRID_W, HEAD_PAIR), 1)
    first_head = lane < HEAD_DIM

    def band_keys(x):
        return [jnp.concatenate([x[a * GRID_W + t * BAND_W:a * GRID_W + (t + 1) * BAND_W]
                                 for a in range(WIN_ROWS)], axis=0) for t in range(N_BANDS)]

    def per_head(x):
        return x[:BAND_Q], x[BAND_Q:]

    def widen(x, t, fill):
        lo, hi = BAND_Q_RANGE[t]
        parts = []
        if lo:
            parts.append(jnp.full((lo,) + x.shape[1:], fill, x.dtype))
        parts.append(x)
        if hi < GRID_W:
            parts.append(jnp.full((GRID_W - hi,) + x.shape[1:], fill, x.dtype))
        return jnp.concatenate(parts, axis=0)

    def scores(i):
        r = rc * rq + i
        rs = jnp.clip(r - WIN_ROWS // 2, 0, rows - WIN_ROWS)
        k0 = pl.multiple_of(rs * GRID_W, GRID_W)
        q2 = q_ref[0, i * GRID_W:(i + 1) * GRID_W, :]
        zero = jnp.zeros_like(q2)
        qa = jnp.where(first_head, q2, zero)
        qb = jnp.where(first_head, zero, q2)
        kb = band_keys(k_ref[0, pl.ds(k0, KEY_ROWS), :])
        ss = []
        for t, (lo, hi) in enumerate(BAND_Q_RANGE):
            lhs = jnp.concatenate([qa[lo:hi], qb[lo:hi]], axis=0)
            s = lax.dot_general(lhs, kb[t], (((1,), (1,)), ((), ())), preferred_element_type=F32)
            ss.append(s + bias_ref[0, r - rs, t])
        return i, k0, ss

    def finish(item):
        i, k0, ss = item
        def row_stat(x):
            return jnp.broadcast_to(x, (x.shape[0], HEAD_PAIR))

        ms = [per_head(row_stat(jnp.max(s, axis=-1, keepdims=True))) for s in ss]
        m_all = [functools.reduce(jnp.maximum, [widen(ms[t][hh], t, NEG_INF) for t in range(N_BANDS)])
                 for hh in range(2)]
        ps, l_all = [], [0.0, 0.0]
        for t, (lo, hi) in enumerate(BAND_Q_RANGE):
            m_t = jnp.concatenate([m_all[0][lo:hi], m_all[1][lo:hi]], axis=0)
            m_t = jnp.concatenate([m_t] * (BAND_KEYS // HEAD_PAIR), axis=1)
            p = jnp.exp2(ss[t] - m_t)
            ls = per_head(row_stat(jnp.sum(p, axis=-1, keepdims=True)))
            for hh in range(2):
                l_all[hh] = l_all[hh] + widen(ls[hh], t, 0.0)
            ps.append(p.astype(BF16))
        vb = band_keys(v_ref[0, pl.ds(k0, KEY_ROWS), :])
        o_all = [0.0, 0.0]
        for t in range(N_BANDS):
            os_ = per_head(jnp.dot(ps[t], vb[t], preferred_element_type=F32))
            for hh in range(2):
                o_all[hh] = o_all[hh] + widen(os_[hh], t, 0.0)
        o2 = jnp.where(first_head, o_all[0] / l_all[0], o_all[1] / l_all[1])
        o_ref[0, i * GRID_W:(i + 1) * GRID_W, :] = o2.astype(BF16)

    pending = [scores(i) for i in range(min(ATT_AHEAD, rq))]
    for i in range(rq):
        if i + ATT_AHEAD < rq:
            pending.append(scores(i + ATT_AHEAD))
        finish(pending.pop(0))


def _attention(q3, k3, v3, bias):
    b, s, _ = q3.shape
    rows = s // GRID_W
    rq = ATT_ROWS
    assert s % GRID_W == 0 and rows % rq == 0 and rows >= WIN_ROWS, (s, rq)
    nhp = N_HEADS // 2
    kern = functools.partial(_attn_kernel, rows=rows, rq=rq)
    return pl.pallas_call(
        kern,
        grid=(b, nhp, rows // rq),
        in_specs=[pl.BlockSpec((1, rq * GRID_W, HEAD_PAIR), lambda bi, hp, rc: (bi, rc, hp)),
                  pl.BlockSpec((1, s, HEAD_PAIR), lambda bi, hp, rc: (bi, 0, hp)),
                  pl.BlockSpec((1, s, HEAD_PAIR), lambda bi, hp, rc: (bi, 0, hp)),
                  pl.BlockSpec((1, WIN_ROWS, N_BANDS, 2 * BAND_Q, BAND_KEYS),
                               lambda bi, hp, rc: (hp, 0, 0, 0, 0))],
        out_specs=pl.BlockSpec((1, rq * GRID_W, HEAD_PAIR), lambda bi, hp, rc: (bi, rc, hp)),
        out_shape=jax.ShapeDtypeStruct((b, s, ATT_WIDTH), BF16),
        compiler_params=pltpu.CompilerParams(
            dimension_semantics=("parallel", "parallel", "arbitrary"),
            vmem_limit_bytes=V7X_VMEM_LIMIT),
        name="nbr_attention",
    )(q3, k3, v3, bias)


def _post_kernel(f_ref, a_ref, x_ref, wf_ref, wa_ref, g1_ref, b1_ref, wg_ref, wu_ref, wd_ref,
                 g2_ref, b2_ref, *rest, fuse_next):
    if fuse_next:
        wn_ref, o_ref, u_ref, q_ref, k_ref, v_ref, x1b_ref = rest
    else:
        o_ref, x1b_ref = rest
    subs = [slice(r, r + POST_SUB) for r in range(0, x_ref.shape[0], POST_SUB)]
    for rows in subs:
        mix = jnp.dot(f_ref[rows, :], wf_ref[...], preferred_element_type=F32)
        mix = mix + jnp.dot(a_ref[rows, :], wa_ref[...], preferred_element_type=F32)
        x1 = _layer_norm(ALPHA * x_ref[rows, :] + mix, g1_ref[...], b1_ref[...])
        o_ref[rows, :] = ALPHA * x1
        x1b_ref[rows, :] = x1.astype(BF16)
    outs = []
    for rows in subs:
        xb = x1b_ref[rows, :]
        for lo in range(0, D_FF, FF_CHUNK):
            sl = slice(lo, lo + FF_CHUNK)
            gate = jnp.dot(xb, wg_ref[:, sl], preferred_element_type=F32)
            up = jnp.dot(xb, wu_ref[:, sl], preferred_element_type=F32)
            hid = (gate * jax.nn.sigmoid(gate) * up).astype(BF16)
            o_ref[rows, :] += jnp.dot(hid, wd_ref[sl, :], preferred_element_type=F32)
        out = _layer_norm(o_ref[rows, :], g2_ref[...], b2_ref[...])
        o_ref[rows, :] = out
        outs.append(out.astype(BF16))
    if fuse_next:
        for rows, ob in zip(subs, outs):
            lo = 0
            for h_ref, scale in ((u_ref, None), (q_ref, ATT_SCALE), (k_ref, None), (v_ref, None)):
                width = h_ref.shape[-1]
                h = jnp.dot(ob, wn_ref[:, lo:lo + width], preferred_element_type=F32)
                if scale is not None:
                    h = h * scale
                h_ref[rows, :] = h.astype(BF16)
                lo += width


def _post(f2d, a2d, x2, p, w_b, layer, fuse_next):
    t = x2.shape[0]
    tm = TM_POST
    assert t % tm == 0 and tm % POST_SUB == 0 and D_FF % FF_CHUNK == 0, (t, tm)
    row = pl.BlockSpec((None, 1, D_MODEL), lambda i: (layer, 0, 0))
    resident = pl.Buffered(1)
    tile = lambda width: pl.BlockSpec((tm, width), lambda i: (i, 0))
    widths = (FOURIER_WIDTH, ATT_WIDTH, ATT_WIDTH, ATT_WIDTH)
    in_specs = [tile(FOURIER_WIDTH), tile(ATT_WIDTH), tile(D_MODEL),
                pl.BlockSpec((FOURIER_WIDTH, D_MODEL), lambda i: (0, 0), pipeline_mode=resident),
                pl.BlockSpec((ATT_WIDTH, D_MODEL), lambda i: (1, 0), pipeline_mode=resident),
                row, row,
                pl.BlockSpec((D_MODEL, D_FF), lambda i: (0, 0), pipeline_mode=resident),
                pl.BlockSpec((D_MODEL, D_FF), lambda i: (0, 0), pipeline_mode=resident),
                pl.BlockSpec((D_FF, D_MODEL), lambda i: (0, 0), pipeline_mode=resident),
                row, row]
    args = [f2d, a2d, x2, w_b["w_out"], w_b["w_out"], p["ln1_g"], p["ln1_b"],
            w_b["w_gate"], w_b["w_up"], w_b["w_down"], p["ln2_g"], p["ln2_b"]]
    out_specs = [tile(D_MODEL)]
    out_shape = [jax.ShapeDtypeStruct((t, D_MODEL), F32)]
    if fuse_next:
        in_specs.append(pl.BlockSpec((None, D_MODEL, IN_WIDTH), lambda i: (layer + 1, 0, 0),
                                     pipeline_mode=resident))
        args.append(p["w_in"])
        out_specs += [tile(wd) for wd in widths]
        out_shape += [jax.ShapeDtypeStruct((t, wd), BF16) for wd in widths]
    outs = pl.pallas_call(
        functools.partial(_post_kernel, fuse_next=fuse_next),
        grid=(t // tm,),
        in_specs=in_specs,
        out_specs=out_specs,
        out_shape=out_shape,
        scratch_shapes=[pltpu.VMEM((tm, D_MODEL), BF16)],
        compiler_params=pltpu.CompilerParams(
            dimension_semantics=("parallel",), vmem_limit_bytes=V7X_VMEM_LIMIT),
        name="post_ffn",
    )(*args)
    return outs[0], (tuple(outs[1:]) if fuse_next else None)


CAST_IN_FOURIER = ("w_out", "w_gate", "w_up", "w_down")


def _layer(x3, p, w_b, layer, uqkv):
    b, s, d = x3.shape
    f1, f2, cs = _dft_constants(s)
    f1_b = jnp.asarray(f1).astype(BF16)
    f2_b = jnp.asarray(f2).astype(BF16)
    cs_b = jnp.asarray(cs).astype(BF16)

    x2d = x3.reshape(b * s, d)
    u, q, k, v = uqkv if uqkv is not None else _in_proj(x2d, p["w_in"], layer)
    to_cast = () if w_b is not None else tuple(p[n] for n in CAST_IN_FOURIER)
    fmix, cast = _fourier_mix(u.reshape(b, s, FOURIER_WIDTH), f1_b, f2_b, cs_b, to_cast, layer)
    if w_b is None:
        w_b = dict(zip(CAST_IN_FOURIER, cast))
    att = _attention(q.reshape(b, s, ATT_WIDTH), k.reshape(b, s, ATT_WIDTH),
                     v.reshape(b, s, ATT_WIDTH), p["bias"])
    x2, nxt = _post(fmix.reshape(b * s, FOURIER_WIDTH), att.reshape(b * s, ATT_WIDTH), x2d, p, w_b,
                    layer, fuse_next=layer + 1 < DEPTH)
    return x2.reshape(b, s, d), w_b, nxt


def kernel(x_prompt, x_sample, w_in, w_out, rpb, ln1_g, ln1_b, w_gate, w_up, w_down, ln2_g, ln2_b):
    p = dict(
        w_in=w_in.astype(BF16), w_out=w_out, w_gate=w_gate, w_up=w_up, w_down=w_down,
        ln1_g=ln1_g.reshape(DEPTH, 1, D_MODEL), ln1_b=ln1_b.reshape(DEPTH, 1, D_MODEL),
        ln2_g=ln2_g.reshape(DEPTH, 1, D_MODEL), ln2_b=ln2_b.reshape(DEPTH, 1, D_MODEL),
    )
    y_prompt, y_sample = x_prompt, x_sample
    uqkv_prompt = uqkv_sample = None
    for layer in range(DEPTH):
        p["bias"] = _bias_tables(rpb, layer)
        y_prompt, w_b, uqkv_prompt = _layer(y_prompt, p, None, layer, uqkv_prompt)
        y_sample, _, uqkv_sample = _layer(y_sample, p, w_b, layer, uqkv_sample)
    return (y_prompt, y_sample)
```

```python
import functools
import math

import jax
import jax.numpy as jnp
import numpy as np
from jax import lax
from jax.experimental import pallas as pl
from jax.experimental.pallas import tpu as pltpu

F32 = jnp.float32
BF16 = jnp.bfloat16

D_MODEL = 1024
DEPTH = 2
FOURIER_WIDTH = 512
FGROUP = 128
N_FGROUPS = FOURIER_WIDTH // FGROUP
ATT_WIDTH = 512
HEAD_DIM = 64
N_HEADS = ATT_WIDTH // HEAD_DIM
IN_WIDTH = FOURIER_WIDTH + 3 * ATT_WIDTH
GRID_W = 64
WIN_ROWS = 8
WIN_COLS = 16
D_FF = 2816
ALPHA = (2 * DEPTH) ** 0.25
LN_EPS = 1e-5
NEG_INF = -1e30
LOG2_E = math.log2(math.e)
ATT_SCALE = HEAD_DIM ** -0.5 * LOG2_E

DFT_N1 = 128
HEAD_PAIR = 2 * HEAD_DIM
KEY_ROWS = WIN_ROWS * GRID_W

V7X_VMEM_LIMIT = 56 * 1024 * 1024

TM_PROJ = 1024
TM_POST = 1024
POST_SUB = 512
FF_CHUNK = 256
FOURIER_ROW_PAD = 8
FOURIER_COPY_UNROLL = 64
FOURIER_UNROLL = 32
ATT_ROWS = 128
ATT_AHEAD = 3


def _layer_norm(z, g, b):
    mu = jnp.mean(z, axis=-1, keepdims=True)
    zc = z - mu
    var = jnp.mean(zc * zc, axis=-1, keepdims=True)
    return zc * lax.rsqrt(var + LN_EPS) * g + b


@functools.lru_cache(maxsize=None)
def _dft_constants(seq):
    n1 = DFT_N1
    n2 = seq // n1
    n = np.arange(n1)[None, None, :] * n2 + np.arange(n2)[:, None, None]
    m = (np.arange(n1)[None, :, None] * n) % seq
    th = 2.0 * np.pi * m / seq
    f1 = np.concatenate([np.cos(th), -np.sin(th)], axis=1) / math.sqrt(n1)
    kn = np.outer(np.arange(n2), np.arange(n2)) % n2
    th = 2.0 * np.pi * kn / n2
    c2, s2 = np.cos(th) / math.sqrt(n2), np.sin(th) / math.sqrt(n2)
    f2 = np.concatenate([np.concatenate([c2, s2], axis=1),
                         np.concatenate([-s2, c2], axis=1)], axis=0)
    cn = np.outer(np.arange(FGROUP), np.arange(FGROUP)) % FGROUP
    th = 2.0 * np.pi * cn / FGROUP
    cs = np.concatenate([np.cos(th), np.sin(th)], axis=0) / math.sqrt(FGROUP)
    return f1.astype(np.float32), f2.astype(np.float32), cs.astype(np.float32)


N_DR = 2 * WIN_ROWS - 1
N_DC = 2 * WIN_COLS - 1

BAND_W = 32
N_BANDS = GRID_W // BAND_W
BAND_KEYS = WIN_ROWS * BAND_W
QUERY_ALIGN = 16


def _band_query_ranges():
    c = np.arange(GRID_W)
    cs = np.clip(c - WIN_COLS // 2, 0, GRID_W - WIN_COLS)
    out = []
    for t in range(N_BANDS):
        sees = c[(cs < (t + 1) * BAND_W) & (cs + WIN_COLS > t * BAND_W)]
        lo = int(sees.min()) // QUERY_ALIGN * QUERY_ALIGN
        hi = -(-(int(sees.max()) + 1) // QUERY_ALIGN) * QUERY_ALIGN
        out.append((lo, hi))
    return out


BAND_Q_RANGE = _band_query_ranges()
BAND_Q = BAND_Q_RANGE[0][1] - BAND_Q_RANGE[0][0]
assert all(hi - lo == BAND_Q for lo, hi in BAND_Q_RANGE)


def _bias_kernel(rpb_ref, o_ref, *, layer):
    h = pl.program_id(0)
    c = lax.broadcasted_iota(jnp.int32, (GRID_W, HEAD_PAIR), 0)
    lane = lax.broadcasted_iota(jnp.int32, (GRID_W, HEAD_PAIR), 1)
    cs = jnp.clip(c - WIN_COLS // 2, 0, GRID_W - WIN_COLS)
    lane_w = lax.broadcasted_iota(jnp.int32, (8, HEAD_PAIR), 1) % GRID_W
    offsets = (0, BAND_W)
    valid = []
    for off in offsets:
        w = (lane - off) % GRID_W
        valid.append((w >= cs) & (w < cs + WIN_COLS))
    for dr in range(N_DR):
        base = ((layer * N_HEADS + h) * N_DR + dr) * N_DC
        vec = jnp.zeros((8, HEAD_PAIR), F32)
        for dc in range(N_DC):
            vec = jnp.where(lane_w == dc, rpb_ref[base + dc] * LOG2_E, vec)
        rep = jnp.concatenate([vec] * (GRID_W // 8), axis=0)
        tiles = []
        for off, ok in zip(offsets, valid):
            t_off = pltpu.roll(rep, (HEAD_PAIR - (WIN_COLS - 1) + off) % HEAD_PAIR, 1,
                               stride=1, stride_axis=0)
            tiles.append(jnp.where(ok, t_off, NEG_INF))
        for d in range(WIN_ROWS):
            a = dr - (WIN_ROWS - 1) + d
            if not 0 <= a < WIN_ROWS:
                continue
            dst = (a * BAND_W) % HEAD_PAIR
            for t, (lo, hi) in enumerate(BAND_Q_RANGE):
                tile = tiles[offsets.index((dst - t * BAND_W) % GRID_W)]
                o_ref[0, d, t, :, a * BAND_W:(a + 1) * BAND_W] = tile[lo:hi, dst:dst + BAND_W]


def _bias_tables(rpb, layer):
    return pl.pallas_call(
        functools.partial(_bias_kernel, layer=layer),
        grid=(N_HEADS,),
        in_specs=[pl.BlockSpec(memory_space=pltpu.SMEM)],
        out_specs=pl.BlockSpec((1, WIN_ROWS, N_BANDS, BAND_Q, BAND_KEYS),
                               lambda h: (h // 2, 0, 0, h % 2, 0)),
        out_shape=jax.ShapeDtypeStruct((N_HEADS // 2, WIN_ROWS, N_BANDS, 2 * BAND_Q, BAND_KEYS), F32),
        compiler_params=pltpu.CompilerParams(dimension_semantics=("parallel",)),
        name="bias_expand",
    )(rpb.astype(F32).reshape(-1))


def _in_proj_kernel(x_ref, w_ref, u_ref, q_ref, k_ref, v_ref):
    xb = x_ref[...].astype(BF16)
    lo = 0
    for o_ref, scale in ((u_ref, None), (q_ref, ATT_SCALE), (k_ref, None), (v_ref, None)):
        width = o_ref.shape[-1]
        h = jnp.dot(xb, w_ref[:, lo:lo + width].astype(BF16), preferred_element_type=F32)
        if scale is not None:
            h = h * scale
        o_ref[...] = h.astype(BF16)
        lo += width


def _in_proj(x2, w_in, layer):
    t = x2.shape[0]
    tm = TM_PROJ
    assert t % tm == 0, (t, tm)
    widths = (FOURIER_WIDTH, ATT_WIDTH, ATT_WIDTH, ATT_WIDTH)
    return pl.pallas_call(
        _in_proj_kernel,
        grid=(t // tm,),
        in_specs=[pl.BlockSpec((tm, D_MODEL), lambda i: (i, 0)),
                  pl.BlockSpec((None, D_MODEL, IN_WIDTH), lambda i: (layer, 0, 0),
                               pipeline_mode=pl.Buffered(1))],
        out_specs=[pl.BlockSpec((tm, wd), lambda i: (i, 0)) for wd in widths],
        out_shape=[jax.ShapeDtypeStruct((t, wd), BF16) for wd in widths],
        compiler_params=pltpu.CompilerParams(
            dimension_semantics=("parallel",), vmem_limit_bytes=V7X_VMEM_LIMIT),
        name="in_proj",
    )(x2, w_in)


def _fourier_kernel(u_ref, f1_ref, f2_ref, cs_ref, *rest, n1, n2, n_cast):
    w_in_refs = rest[:n_cast]
    o_ref = rest[n_cast]
    w_out_refs = rest[n_cast + 1:2 * n_cast + 1]
    tok_ref, y_ref = rest[2 * n_cast + 1:]
    for w_src, w_dst in zip(w_in_refs, w_out_refs):
        w_dst[...] = w_src[...].astype(BF16)
    grp = FOURIER_UNROLL
    tp = n2 + FOURIER_ROW_PAD
    yp = 2 * n1 + FOURIER_ROW_PAD

    def stage_in(i, carry):
        src = pl.multiple_of(i * n2, n2)
        dst = pl.multiple_of(i * tp, 8)
        tok_ref[pl.ds(dst, n2), :] = u_ref[0, pl.ds(src, n2), :].astype(F32)
        return carry

    lax.fori_loop(0, n1, stage_in, 0, unroll=FOURIER_COPY_UNROLL)

    def stage1(nn, carry):
        xs = tok_ref[pl.ds(nn, n1, stride=tp), :].astype(BF16)
        z = jnp.dot(f1_ref[nn], xs, preferred_element_type=F32)
        y_ref[pl.ds(pl.multiple_of(nn * yp, 8), 2 * n1), :] = z
        return carry

    lax.fori_loop(0, n2, stage1, 0, unroll=FOURIER_COPY_UNROLL)

    def stage2(t, carry):
        zs = []
        for j in range(grp):
            k1 = t * grp + j
            zr = y_ref[pl.ds(k1, n2, stride=yp), :]
            zi = y_ref[pl.ds(n1 + k1, n2, stride=yp), :]
            zs.append(jnp.concatenate([zr, zi], axis=0).astype(BF16))
        xs = []
        for j in range(0, grp, 2):
            z2 = jnp.concatenate([zs[j], zs[j + 1]], axis=1)
            x2 = jnp.dot(f2_ref[...], z2, preferred_element_type=F32).astype(BF16)
            xs.extend([x2[:, :FGROUP], x2[:, FGROUP:]])
        for j, x in enumerate(xs):
            k1 = t * grp + j
            xc = jnp.concatenate([x[:n2], x[n2:]], axis=1)
            o = jnp.dot(xc, cs_ref[...], preferred_element_type=F32)
            tok_ref[pl.ds(pl.multiple_of(k1 * tp, 8), n2), :] = o
        return carry

    for t in range(n1 // grp):
        stage2(t, 0)

    def stage_out(k2, carry):
        o = tok_ref[pl.ds(k2, n1, stride=tp), :]
        o_ref[0, pl.ds(pl.multiple_of(k2 * n1, n1), n1), :] = o.astype(BF16)
        return carry

    lax.fori_loop(0, n2, stage_out, 0, unroll=FOURIER_COPY_UNROLL)


def _fourier_mix(u3, f1_b, f2_b, cs_b, cast_weights=()):
    b, s, _ = u3.shape
    n1 = DFT_N1
    n2 = s // n1
    assert s == n1 * n2 and n2 % QUERY_ALIGN == 0 and n1 % FOURIER_UNROLL == 0, (s, n1, n2)
    n_steps = b * N_FGROUPS
    step = lambda bi, g: bi * N_FGROUPS + g
    cast_in, cast_out, cast_shapes = [], [], []
    for w, layer in cast_weights:
        _, rows, cols = w.shape
        slab = rows // n_steps
        assert rows % n_steps == 0 and slab % QUERY_ALIGN == 0, (w.shape, n_steps)
        cast_in.append(pl.BlockSpec((None, slab, cols), lambda bi, g, layer=layer: (layer, step(bi, g), 0)))
        cast_out.append(pl.BlockSpec((slab, cols), lambda bi, g: (step(bi, g), 0)))
        cast_shapes.append(jax.ShapeDtypeStruct((rows, cols), BF16))
    kern = functools.partial(_fourier_kernel, n1=n1, n2=n2, n_cast=len(cast_weights))
    outs = pl.pallas_call(
        kern,
        grid=(b, N_FGROUPS),
        in_specs=[pl.BlockSpec((1, s, FGROUP), lambda bi, g: (bi, 0, g)),
                  pl.BlockSpec((n2, 2 * n1, n1), lambda bi, g: (0, 0, 0),
                               pipeline_mode=pl.Buffered(1)),
                  pl.BlockSpec((2 * n2, 2 * n2), lambda bi, g: (0, 0)),
                  pl.BlockSpec((2 * FGROUP, FGROUP), lambda bi, g: (0, 0))] + cast_in,
        out_specs=[pl.BlockSpec((1, s, FGROUP), lambda bi, g: (bi, 0, g))] + cast_out,
        out_shape=[jax.ShapeDtypeStruct((b, s, FOURIER_WIDTH), BF16)] + cast_shapes,
        scratch_shapes=[pltpu.VMEM((n1 * (n2 + FOURIER_ROW_PAD), FGROUP), F32),
                        pltpu.VMEM((n2 * (2 * n1 + FOURIER_ROW_PAD), FGROUP), F32)],
        compiler_params=pltpu.CompilerParams(
            dimension_semantics=("parallel", "parallel"), vmem_limit_bytes=V7X_VMEM_LIMIT),
        name="fourier_mix",
    )(u3, f1_b, f2_b, cs_b, *[w for w, _ in cast_weights])
    return outs[0], tuple(outs[1:])


def _attn_kernel(q_ref, k_ref, v_ref, bias_ref, o_ref, *, rows, rq):
    rc = pl.program_id(2)
    lane = lax.broadcasted_iota(jnp.int32, (GRID_W, HEAD_PAIR), 1)
    first_head = lane < HEAD_DIM

    def band_keys(x):
        return [jnp.concatenate([x[a * GRID_W + t * BAND_W:a * GRID_W + (t + 1) * BAND_W]
                                 for a in range(WIN_ROWS)], axis=0) for t in range(N_BANDS)]

    def per_head(x):
        return x[:BAND_Q], x[BAND_Q:]

    def widen(x, t, fill):
        lo, hi = BAND_Q_RANGE[t]
        parts = []
        if lo:
            parts.append(jnp.full((lo,) + x.shape[1:], fill, x.dtype))
        parts.append(x)
        if hi < GRID_W:
            parts.append(jnp.full((GRID_W - hi,) + x.shape[1:], fill, x.dtype))
        return jnp.concatenate(parts, axis=0)

    def scores(i):
        r = rc * rq + i
        rs = jnp.clip(r - WIN_ROWS // 2, 0, rows - WIN_ROWS)
        k0 = pl.multiple_of(rs * GRID_W, GRID_W)
        q2 = q_ref[0, i * GRID_W:(i + 1) * GRID_W, :]
        zero = jnp.zeros_like(q2)
        qa = jnp.where(first_head, q2, zero)
        qb = jnp.where(first_head, zero, q2)
        kb = band_keys(k_ref[0, pl.ds(k0, KEY_ROWS), :])
        ss = []
        for t, (lo, hi) in enumerate(BAND_Q_RANGE):
            lhs = jnp.concatenate([qa[lo:hi], qb[lo:hi]], axis=0)
            s = lax.dot_general(lhs, kb[t], (((1,), (1,)), ((), ())), preferred_element_type=F32)
            ss.append(s + bias_ref[0, r - rs, t])
        return i, k0, ss

    def finish(item):
        i, k0, ss = item
        def row_stat(x):
            return jnp.broadcast_to(x, (x.shape[0], HEAD_PAIR))

        ms = [per_head(row_stat(jnp.max(s, axis=-1, keepdims=True))) for s in ss]
        m_all = [functools.reduce(jnp.maximum, [widen(ms[t][hh], t, NEG_INF) for t in range(N_BANDS)])
                 for hh in range(2)]
        ps, l_all = [], [0.0, 0.0]
        for t, (lo, hi) in enumerate(BAND_Q_RANGE):
            m_t = jnp.concatenate([m_all[0][lo:hi], m_all[1][lo:hi]], axis=0)
            m_t = jnp.concatenate([m_t] * (BAND_KEYS // HEAD_PAIR), axis=1)
            p = jnp.exp2(ss[t] - m_t)
            ls = per_head(row_stat(jnp.sum(p, axis=-1, keepdims=True)))
            for hh in range(2):
                l_all[hh] = l_all[hh] + widen(ls[hh], t, 0.0)
            ps.append(p.astype(BF16))
        vb = band_keys(v_ref[0, pl.ds(k0, KEY_ROWS), :])
        o_all = [0.0, 0.0]
        for t in range(N_BANDS):
            os_ = per_head(jnp.dot(ps[t], vb[t], preferred_element_type=F32))
            for hh in range(2):
                o_all[hh] = o_all[hh] + widen(os_[hh], t, 0.0)
        o2 = jnp.where(first_head, o_all[0] / l_all[0], o_all[1] / l_all[1])
        o_ref[0, i * GRID_W:(i + 1) * GRID_W, :] = o2.astype(BF16)

    pending = [scores(i) for i in range(min(ATT_AHEAD, rq))]
    for i in range(rq):
        if i + ATT_AHEAD < rq:
            pending.append(scores(i + ATT_AHEAD))
        finish(pending.pop(0))


def _attention(q3, k3, v3, bias):
    b, s, _ = q3.shape
    rows = s // GRID_W
    rq = ATT_ROWS
    assert s % GRID_W == 0 and rows % rq == 0 and rows >= WIN_ROWS, (s, rq)
    nhp = N_HEADS // 2
    kern = functools.partial(_attn_kernel, rows=rows, rq=rq)
    return pl.pallas_call(
        kern,
        grid=(b, nhp, rows // rq),
        in_specs=[pl.BlockSpec((1, rq * GRID_W, HEAD_PAIR), lambda bi, hp, rc: (bi, rc, hp)),
                  pl.BlockSpec((1, s, HEAD_PAIR), lambda bi, hp, rc: (bi, 0, hp)),
                  pl.BlockSpec((1, s, HEAD_PAIR), lambda bi, hp, rc: (bi, 0, hp)),
                  pl.BlockSpec((1, WIN_ROWS, N_BANDS, 2 * BAND_Q, BAND_KEYS),
                               lambda bi, hp, rc: (hp, 0, 0, 0, 0))],
        out_specs=pl.BlockSpec((1, rq * GRID_W, HEAD_PAIR), lambda bi, hp, rc: (bi, rc, hp)),
        out_shape=jax.ShapeDtypeStruct((b, s, ATT_WIDTH), BF16),
        compiler_params=pltpu.CompilerParams(
            dimension_semantics=("parallel", "parallel", "arbitrary"),
            vmem_limit_bytes=V7X_VMEM_LIMIT),
        name="nbr_attention",
    )(q3, k3, v3, bias)


def _post_kernel(f_ref, a_ref, x_ref, wf_ref, wa_ref, g1_ref, b1_ref, wg_ref, wu_ref, wd_ref,
                 g2_ref, b2_ref, *rest, fuse_next):
    if fuse_next:
        wn_ref, o_ref, u_ref, q_ref, k_ref, v_ref, x1b_ref = rest
    else:
        o_ref, x1b_ref = rest
    subs = [slice(r, r + POST_SUB) for r in range(0, x_ref.shape[0], POST_SUB)]
    for rows in subs:
        mix = jnp.dot(f_ref[rows, :], wf_ref[...], preferred_element_type=F32)
        mix = mix + jnp.dot(a_ref[rows, :], wa_ref[...], preferred_element_type=F32)
        x1 = _layer_norm(ALPHA * x_ref[rows, :] + mix, g1_ref[...], b1_ref[...])
        o_ref[rows, :] = ALPHA * x1
        x1b_ref[rows, :] = x1.astype(BF16)
    outs = []
    for rows in subs:
        xb = x1b_ref[rows, :]
        for lo in range(0, D_FF, FF_CHUNK):
            sl = slice(lo, lo + FF_CHUNK)
            gate = jnp.dot(xb, wg_ref[:, sl], preferred_element_type=F32)
            up = jnp.dot(xb, wu_ref[:, sl], preferred_element_type=F32)
            hid = (gate * jax.nn.sigmoid(gate) * up).astype(BF16)
            o_ref[rows, :] += jnp.dot(hid, wd_ref[sl, :], preferred_element_type=F32)
        out = _layer_norm(o_ref[rows, :], g2_ref[...], b2_ref[...])
        o_ref[rows, :] = out
        outs.append(out.astype(BF16))
    if fuse_next:
        for rows, ob in zip(subs, outs):
            lo = 0
            for h_ref, scale in ((u_ref, None), (q_ref, ATT_SCALE), (k_ref, None), (v_ref, None)):
                width = h_ref.shape[-1]
                h = jnp.dot(ob, wn_ref[:, lo:lo + width], preferred_element_type=F32)
                if scale is not None:
                    h = h * scale
                h_ref[rows, :] = h.astype(BF16)
                lo += width


def _post(f2d, a2d, x2, p, w_b, layer, fuse_next):
    t = x2.shape[0]
    tm = TM_POST
    assert t % tm == 0 and tm % POST_SUB == 0 and D_FF % FF_CHUNK == 0, (t, tm)
    row = pl.BlockSpec((None, 1, D_MODEL), lambda i: (layer, 0, 0))
    resident = pl.Buffered(1)
    tile = lambda width: pl.BlockSpec((tm, width), lambda i: (i, 0))
    widths = (FOURIER_WIDTH, ATT_WIDTH, ATT_WIDTH, ATT_WIDTH)
    in_specs = [tile(FOURIER_WIDTH), tile(ATT_WIDTH), tile(D_MODEL),
                pl.BlockSpec((FOURIER_WIDTH, D_MODEL), lambda i: (0, 0), pipeline_mode=resident),
                pl.BlockSpec((ATT_WIDTH, D_MODEL), lambda i: (1, 0), pipeline_mode=resident),
                row, row,
                pl.BlockSpec((D_MODEL, D_FF), lambda i: (0, 0), pipeline_mode=resident),
                pl.BlockSpec((D_MODEL, D_FF), lambda i: (0, 0), pipeline_mode=resident),
                pl.BlockSpec((D_FF, D_MODEL), lambda i: (0, 0), pipeline_mode=resident),
                row, row]
    args = [f2d, a2d, x2, w_b["w_out"], w_b["w_out"], p["ln1_g"], p["ln1_b"],
            w_b["w_gate"], w_b["w_up"], w_b["w_down"], p["ln2_g"], p["ln2_b"]]
    out_specs = [tile(D_MODEL)]
    out_shape = [jax.ShapeDtypeStruct((t, D_MODEL), F32)]
    if fuse_next:
        in_specs.append(pl.BlockSpec((D_MODEL, IN_WIDTH), lambda i: (0, 0), pipeline_mode=resident))
        args.append(w_b["w_in_next"])
        out_specs += [tile(wd) for wd in widths]
        out_shape += [jax.ShapeDtypeStruct((t, wd), BF16) for wd in widths]
    outs = pl.pallas_call(
        functools.partial(_post_kernel, fuse_next=fuse_next),
        grid=(t // tm,),
        in_specs=in_specs,
        out_specs=out_specs,
        out_shape=out_shape,
        scratch_shapes=[pltpu.VMEM((tm, D_MODEL), BF16)],
        compiler_params=pltpu.CompilerParams(
            dimension_semantics=("parallel",), vmem_limit_bytes=V7X_VMEM_LIMIT),
        name="post_ffn",
    )(*args)
    return outs[0], (tuple(outs[1:]) if fuse_next else None)


CAST_IN_FOURIER = ("w_out", "w_gate", "w_up", "w_down")


def _layer(x3, p, w_b, layer, uqkv):
    b, s, d = x3.shape
    f1, f2, cs = _dft_constants(s)
    f1_b = jnp.asarray(f1).astype(BF16)
    f2_b = jnp.asarray(f2).astype(BF16)
    cs_b = jnp.asarray(cs).astype(BF16)

    x2d = x3.reshape(b * s, d)
    u, q, k, v = uqkv if uqkv is not None else _in_proj(x2d, p["w_in"], layer)
    fuse_next = layer + 1 < DEPTH
    names, to_cast = (), ()
    if w_b is None:
        names = CAST_IN_FOURIER + (("w_in_next",) if fuse_next else ())
        to_cast = tuple((p[n], layer) for n in CAST_IN_FOURIER)
        to_cast += ((p["w_in"], layer + 1),) if fuse_next else ()
    fmix, cast = _fourier_mix(u.reshape(b, s, FOURIER_WIDTH), f1_b, f2_b, cs_b, to_cast)
    if w_b is None:
        w_b = dict(zip(names, cast))
    att = _attention(q.reshape(b, s, ATT_WIDTH), k.reshape(b, s, ATT_WIDTH),
                     v.reshape(b, s, ATT_WIDTH), p["bias"])
    x2, nxt = _post(fmix.reshape(b * s, FOURIER_WIDTH), att.reshape(b * s, ATT_WIDTH), x2d, p, w_b,
                    layer, fuse_next=fuse_next)
    return x2.reshape(b, s, d), w_b, nxt


def kernel(x_prompt, x_sample, w_in, w_out, rpb, ln1_g, ln1_b, w_gate, w_up, w_down, ln2_g, ln2_b):
    p = dict(
        w_in=w_in, w_out=w_out, w_gate=w_gate, w_up=w_up, w_down=w_down,
        ln1_g=ln1_g.reshape(DEPTH, 1, D_MODEL), ln1_b=ln1_b.reshape(DEPTH, 1, D_MODEL),
        ln2_g=ln2_g.reshape(DEPTH, 1, D_MODEL), ln2_b=ln2_b.reshape(DEPTH, 1, D_MODEL),
    )
    y_prompt, y_sample = x_prompt, x_sample
    uqkv_prompt = uqkv_sample = None
    for layer in range(DEPTH):
        p["bias"] = _bias_tables(rpb, layer)
        y_prompt, w_b, uqkv_prompt = _layer(y_prompt, p, None, layer, uqkv_prompt)
        y_sample, _, uqkv_sample = _layer(y_sample, p, w_b, layer, uqkv_sample)
    return (y_prompt, y_sample)
```

```python
import functools
import math

import jax
import jax.numpy as jnp
import numpy as np
from jax import lax
from jax.experimental import pallas as pl
from jax.experimental.pallas import tpu as pltpu

F32 = jnp.float32
BF16 = jnp.bfloat16

D_MODEL = 1024
DEPTH = 2
FOURIER_WIDTH = 512
FGROUP = 128
N_FGROUPS = FOURIER_WIDTH // FGROUP
ATT_WIDTH = 512
HEAD_DIM = 64
N_HEADS = ATT_WIDTH // HEAD_DIM
IN_WIDTH = FOURIER_WIDTH + 3 * ATT_WIDTH
GRID_W = 64
WIN_ROWS = 8
WIN_COLS = 16
D_FF = 2816
ALPHA = (2 * DEPTH) ** 0.25
LN_EPS = 1e-5
NEG_INF = -1e30
LOG2_E = math.log2(math.e)
ATT_SCALE = HEAD_DIM ** -0.5 * LOG2_E

DFT_N1 = 128
HEAD_PAIR = 2 * HEAD_DIM
KEY_ROWS = WIN_ROWS * GRID_W

V7X_VMEM_LIMIT = 56 * 1024 * 1024

TM_PROJ = 1024
TM_POST = 1024
POST_SUB = 512
FF_CHUNK = 256
FOURIER_ROW_PAD = 8
FOURIER_COPY_UNROLL = 64
FOURIER_UNROLL = 32
ATT_ROWS = 128
ATT_AHEAD = 3


def _layer_norm(z, g, b):
    mu = jnp.mean(z, axis=-1, keepdims=True)
    zc = z - mu
    var = jnp.mean(zc * zc, axis=-1, keepdims=True)
    return zc * lax.rsqrt(var + LN_EPS) * g + b


@functools.lru_cache(maxsize=None)
def _dft_constants(seq):
    n1 = DFT_N1
    n2 = seq // n1
    n = np.arange(n1)[None, None, :] * n2 + np.arange(n2)[:, None, None]
    m = (np.arange(n1)[None, :, None] * n) % seq
    th = 2.0 * np.pi * m / seq
    f1 = np.concatenate([np.cos(th), -np.sin(th)], axis=1) / math.sqrt(n1)
    kn = np.outer(np.arange(n2), np.arange(n2)) % n2
    th = 2.0 * np.pi * kn / n2
    c2, s2 = np.cos(th) / math.sqrt(n2), np.sin(th) / math.sqrt(n2)
    f2 = np.concatenate([np.concatenate([c2, s2], axis=1),
                         np.concatenate([-s2, c2], axis=1)], axis=0)
    cn = np.outer(np.arange(FGROUP), np.arange(FGROUP)) % FGROUP
    th = 2.0 * np.pi * cn / FGROUP
    cs = np.concatenate([np.cos(th), np.sin(th)], axis=0) / math.sqrt(FGROUP)
    return f1.astype(np.float32), f2.astype(np.float32), cs.astype(np.float32)


N_DR = 2 * WIN_ROWS - 1
N_DC = 2 * WIN_COLS - 1

BAND_W = 32
N_BANDS = GRID_W // BAND_W
BAND_KEYS = WIN_ROWS * BAND_W
QUERY_ALIGN = 16


def _band_query_ranges():
    c = np.arange(GRID_W)
    cs = np.clip(c - WIN_COLS // 2, 0, GRID_W - WIN_COLS)
    out = []
    for t in range(N_BANDS):
        sees = c[(cs < (t + 1) * BAND_W) & (cs + WIN_COLS > t * BAND_W)]
        lo = int(sees.min()) // QUERY_ALIGN * QUERY_ALIGN
        hi = -(-(int(sees.max()) + 1) // QUERY_ALIGN) * QUERY_ALIGN
        out.append((lo, hi))
    return out


BAND_Q_RANGE = _band_query_ranges()
BAND_Q = BAND_Q_RANGE[0][1] - BAND_Q_RANGE[0][0]
assert all(hi - lo == BAND_Q for lo, hi in BAND_Q_RANGE)


def _bias_kernel(rpb_ref, o_ref):
    layer = pl.program_id(0)
    h = pl.program_id(1)
    c = lax.broadcasted_iota(jnp.int32, (GRID_W, HEAD_PAIR), 0)
    lane =lax.broadcasted_iota(jnp.int32, (GRID_W, HEAD_PAIR), 1)
    cs = jnp.clip(c - WIN_COLS // 2, 0, GRID_W - WIN_COLS)
    lane_w = lax.broadcasted_iota(jnp.int32, (8, HEAD_PAIR), 1) % GRID_W
    offsets = (0, BAND_W)
    valid = []
    for off in offsets:
        w = (lane - off) % GRID_W
        valid.append((w >= cs) & (w < cs + WIN_COLS))
    for dr in range(N_DR):
        base = ((layer * N_HEADS + h) * N_DR + dr) * N_DC
        vec = jnp.zeros((8, HEAD_PAIR), F32)
        for dc in range(N_DC):
            vec = jnp.where(lane_w == dc, rpb_ref[base + dc] * LOG2_E, vec)
        rep = jnp.concatenate([vec] * (GRID_W // 8), axis=0)
        tiles = []
        for off, ok in zip(offsets, valid):
            t_off = pltpu.roll(rep, (HEAD_PAIR - (WIN_COLS - 1) + off) % HEAD_PAIR, 1,
                               stride=1, stride_axis=0)
            tiles.append(jnp.where(ok, t_off, NEG_INF))
        for d in range(WIN_ROWS):
            a = dr - (WIN_ROWS - 1) + d
            if not 0 <= a < WIN_ROWS:
                continue
            dst = (a * BAND_W) % HEAD_PAIR
            for t, (lo, hi) in enumerate(BAND_Q_RANGE):
                tile = tiles[offsets.index((dst - t * BAND_W) % GRID_W)]
                o_ref[0, d, t, :, a * BAND_W:(a + 1) * BAND_W] = tile[lo:hi, dst:dst + BAND_W]


def _bias_tables(rpb):
    return pl.pallas_call(
        _bias_kernel,
        grid=(DEPTH, N_HEADS),
        in_specs=[pl.BlockSpec(memory_space=pltpu.SMEM)],
        out_specs=pl.BlockSpec((None, 1, WIN_ROWS, N_BANDS, BAND_Q, BAND_KEYS),
                               lambda l, h: (l, h // 2, 0, 0, h % 2, 0)),
        out_shape=jax.ShapeDtypeStruct(
            (DEPTH, N_HEADS // 2, WIN_ROWS, N_BANDS, 2 * BAND_Q, BAND_KEYS), F32),
        compiler_params=pltpu.CompilerParams(dimension_semantics=("parallel", "parallel")),
        name="bias_expand",
    )(rpb.astype(F32).reshape(-1))


def _in_proj_kernel(x_ref, w_ref, u_ref, q_ref, k_ref, v_ref):
    xb = x_ref[...].astype(BF16)
    lo = 0
    for o_ref, scale in ((u_ref, None), (q_ref, ATT_SCALE), (k_ref, None), (v_ref, None)):
        width = o_ref.shape[-1]
        h = jnp.dot(xb, w_ref[:, lo:lo + width].astype(BF16), preferred_element_type=F32)
        if scale is not None:
            h = h * scale
        o_ref[...] = h.astype(BF16)
        lo += width


def _in_proj(x2, w_in, layer):
    t = x2.shape[0]
    tm = TM_PROJ
    assert t % tm == 0, (t, tm)
    widths = (FOURIER_WIDTH, ATT_WIDTH, ATT_WIDTH, ATT_WIDTH)
    return pl.pallas_call(
        _in_proj_kernel,
        grid=(t // tm,),
        in_specs=[pl.BlockSpec((tm, D_MODEL), lambda i: (i, 0)),
                  pl.BlockSpec((None, D_MODEL, IN_WIDTH), lambda i: (layer, 0, 0),
                               pipeline_mode=pl.Buffered(1))],
        out_specs=[pl.BlockSpec((tm, wd), lambda i: (i, 0)) for wd in widths],
        out_shape=[jax.ShapeDtypeStruct((t, wd), BF16) for wd in widths],
        compiler_params=pltpu.CompilerParams(
            dimension_semantics=("parallel",), vmem_limit_bytes=V7X_VMEM_LIMIT),
        name="in_proj",
    )(x2, w_in)


def _fourier_kernel(u_ref, f1_ref, f2_ref, cs_ref, *rest, n1, n2, n_cast):
    w_in_refs = rest[:n_cast]
    o_ref = rest[n_cast]
    w_out_refs = rest[n_cast + 1:2 * n_cast + 1]
    tok_ref, y_ref = rest[2 * n_cast + 1:]
    for w_src, w_dst in zip(w_in_refs, w_out_refs):
        w_dst[...] = w_src[...].astype(BF16)
    grp = FOURIER_UNROLL
    tp = n2 + FOURIER_ROW_PAD
    yp = 2 * n1 + FOURIER_ROW_PAD

    def stage_in(i, carry):
        src = pl.multiple_of(i * n2, n2)
        dst = pl.multiple_of(i * tp, 8)
        tok_ref[pl.ds(dst, n2), :] = u_ref[0, pl.ds(src, n2), :].astype(F32)
        return carry

    lax.fori_loop(0, n1, stage_in, 0, unroll=FOURIER_COPY_UNROLL)

    def stage1(nn, carry):
        xs = tok_ref[pl.ds(nn, n1, stride=tp), :].astype(BF16)
        z = jnp.dot(f1_ref[nn], xs, preferred_element_type=F32)
        y_ref[pl.ds(pl.multiple_of(nn * yp, 8), 2 * n1), :] = z
        return carry

    lax.fori_loop(0, n2, stage1, 0, unroll=FOURIER_COPY_UNROLL)

    def stage2(t, carry):
        zs = []
        for j in range(grp):
            k1 = t * grp + j
            zr = y_ref[pl.ds(k1, n2, stride=yp), :]
            zi = y_ref[pl.ds(n1 + k1, n2, stride=yp), :]
            zs.append(jnp.concatenate([zr, zi], axis=0).astype(BF16))
        xs = []
        for j in range(0, grp, 2):
            z2 = jnp.concatenate([zs[j], zs[j + 1]], axis=1)
            x2 = jnp.dot(f2_ref[...], z2, preferred_element_type=F32).astype(BF16)
            xs.extend([x2[:, :FGROUP], x2[:, FGROUP:]])
        for j, x in enumerate(xs):
            k1 = t * grp + j
            xc = jnp.concatenate([x[:n2], x[n2:]], axis=1)
            o = jnp.dot(xc, cs_ref[...], preferred_element_type=F32)
            tok_ref[pl.ds(pl.multiple_of(k1 * tp, 8), n2), :] = o
        return carry

    for t in range(n1 // grp):
        stage2(t, 0)

    def stage_out(k2, carry):
        o = tok_ref[pl.ds(k2, n1, stride=tp), :]
        o_ref[0, pl.ds(pl.multiple_of(k2 * n1, n1), n1), :] = o.astype(BF16)
        return carry

    lax.fori_loop(0, n2, stage_out, 0, unroll=FOURIER_COPY_UNROLL)


def _fourier_mix(u3, f1_b, f2_b, cs_b, cast_weights=()):
    b, s, _ = u3.shape
    n1 = DFT_N1
    n2 = s // n1
    assert s == n1 * n2 and n2 % QUERY_ALIGN == 0 and n1 % FOURIER_UNROLL == 0, (s, n1, n2)
    n_steps = b * N_FGROUPS
    step = lambda bi, g: bi * N_FGROUPS + g
    cast_in, cast_out, cast_shapes = [], [], []
    for w, layer in cast_weights:
        _, rows, cols = w.shape
        slab = rows // n_steps
        assert rows % n_steps == 0 and slab % QUERY_ALIGN == 0, (w.shape, n_steps)
        cast_in.append(pl.BlockSpec((None, slab, cols), lambda bi, g, layer=layer: (layer, step(bi, g), 0)))
        cast_out.append(pl.BlockSpec((slab, cols), lambda bi, g: (step(bi, g), 0)))
        cast_shapes.append(jax.ShapeDtypeStruct((rows, cols), BF16))
    kern = functools.partial(_fourier_kernel, n1=n1, n2=n2, n_cast=len(cast_weights))
    outs = pl.pallas_call(
        kern,
        grid=(b, N_FGROUPS),
        in_specs=[pl.BlockSpec((1, s, FGROUP), lambda bi, g: (bi, 0, g)),
                  pl.BlockSpec((n2, 2 * n1, n1), lambda bi, g: (0, 0, 0),
                               pipeline_mode=pl.Buffered(1)),
                  pl.BlockSpec((2 * n2, 2 * n2), lambda bi, g: (0, 0)),
                  pl.BlockSpec((2 * FGROUP, FGROUP), lambda bi, g: (0, 0))] + cast_in,
        out_specs=[pl.BlockSpec((1, s, FGROUP), lambda bi, g: (bi, 0, g))] + cast_out,
        out_shape=[jax.ShapeDtypeStruct((b, s, FOURIER_WIDTH), BF16)] + cast_shapes,
        scratch_shapes=[pltpu.VMEM((n1 * (n2 + FOURIER_ROW_PAD), FGROUP), F32),
                        pltpu.VMEM((n2 * (2 * n1 + FOURIER_ROW_PAD), FGROUP), F32)],
        compiler_params=pltpu.CompilerParams(
            dimension_semantics=("parallel", "parallel"), vmem_limit_bytes=V7X_VMEM_LIMIT),
        name="fourier_mix",
    )(u3, f1_b, f2_b, cs_b, *[w for w, _ in cast_weights])
    return outs[0], tuple(outs[1:])


def _attn_kernel(q_ref, k_ref, v_ref, bias_ref, o_ref, *, rows, rq):
    rc = pl.program_id(2)
    lane = lax.broadcasted_iota(jnp.int32, (GRID_W, HEAD_PAIR), 1)
    first_head = lane < HEAD_DIM

    def band_keys(x):
        return [jnp.concatenate([x[a * GRID_W + t * BAND_W:a * GRID_W + (t + 1) * BAND_W]
                                 for a in range(WIN_ROWS)], axis=0) for t in range(N_BANDS)]

    def per_head(x):
        return x[:BAND_Q], x[BAND_Q:]

    def widen(x, t, fill):
        lo, hi = BAND_Q_RANGE[t]
        parts = []
        if lo:
            parts.append(jnp.full((lo,) + x.shape[1:], fill, x.dtype))
        parts.append(x)
        if hi < GRID_W:
            parts.append(jnp.full((GRID_W - hi,) + x.shape[1:], fill, x.dtype))
        return jnp.concatenate(parts, axis=0)

    def scores(i):
        r = rc * rq + i
        rs = jnp.clip(r - WIN_ROWS // 2, 0, rows - WIN_ROWS)
        k0 = pl.multiple_of(rs * GRID_W, GRID_W)
        q2 = q_ref[0, i * GRID_W:(i + 1) * GRID_W, :]
        zero = jnp.zeros_like(q2)
        qa = jnp.where(first_head, q2, zero)
        qb = jnp.where(first_head, zero, q2)
        kb = band_keys(k_ref[0, pl.ds(k0, KEY_ROWS), :])
        ss = []
        for t, (lo, hi) in enumerate(BAND_Q_RANGE):
            lhs = jnp.concatenate([qa[lo:hi], qb[lo:hi]], axis=0)
            s = lax.dot_general(lhs, kb[t], (((1,), (1,)), ((), ())), preferred_element_type=F32)
            ss.append(s + bias_ref[0, r - rs, t])
        return i, k0, ss

    def finish(item):
        i, k0, ss = item
        def row_stat(x):
            return jnp.broadcast_to(x, (x.shape[0], HEAD_PAIR))

        ms = [per_head(row_stat(jnp.max(s, axis=-1, keepdims=True))) for s in ss]
        m_all = [functools.reduce(jnp.maximum, [widen(ms[t][hh], t, NEG_INF) for t in range(N_BANDS)])
                 for hh in range(2)]
        ps, l_all = [], [0.0, 0.0]
        for t, (lo, hi) in enumerate(BAND_Q_RANGE):
            m_t = jnp.concatenate([m_all[0][lo:hi], m_all[1][lo:hi]], axis=0)
            m_t = jnp.concatenate([m_t] * (BAND_KEYS // HEAD_PAIR), axis=1)
            p = jnp.exp2(ss[t] - m_t)
            ls = per_head(row_stat(jnp.sum(p, axis=-1, keepdims=True)))
            for hh in range(2):
                l_all[hh] = l_all[hh] + widen(ls[hh], t, 0.0)
            ps.append(p.astype(BF16))
        vb = band_keys(v_ref[0, pl.ds(k0, KEY_ROWS), :])
        o_all = [0.0, 0.0]
        for t in range(N_BANDS):
            os_ = per_head(jnp.dot(ps[t], vb[t], preferred_element_type=F32))
            for hh in range(2):
                o_all[hh] = o_all[hh] + widen(os_[hh], t, 0.0)
        o2 = jnp.where(first_head, o_all[0] / l_all[0], o_all[1] / l_all[1])
        o_ref[0, i * GRID_W:(i + 1) * GRID_W, :] = o2.astype(BF16)

    pending = [scores(i) for i in range(min(ATT_AHEAD, rq))]
    for i in range(rq):
        if i + ATT_AHEAD < rq:
            pending.append(scores(i + ATT_AHEAD))
        finish(pending.pop(0))


def _attention(q3, k3, v3, bias, layer):
    b, s, _ = q3.shape
    rows = s // GRID_W
    rq = ATT_ROWS
    assert s % GRID_W == 0 and rows % rq == 0 and rows >= WIN_ROWS, (s, rq)
    nhp = N_HEADS // 2
    kern = functools.partial(_attn_kernel, rows=rows, rq=rq)
    return pl.pallas_call(
        kern,
        grid=(b, nhp, rows // rq),
        in_specs=[pl.BlockSpec((1, rq * GRID_W, HEAD_PAIR), lambda bi, hp, rc: (bi, rc, hp)),
                  pl.BlockSpec((1, s, HEAD_PAIR), lambda bi, hp, rc: (bi, 0, hp)),
                  pl.BlockSpec((1, s, HEAD_PAIR), lambda bi, hp, rc: (bi, 0, hp)),
                  pl.BlockSpec((None, 1, WIN_ROWS, N_BANDS, 2 * BAND_Q, BAND_KEYS),
                               lambda bi, hp, rc: (layer, hp, 0, 0, 0, 0))],
        out_specs=pl.BlockSpec((1, rq * GRID_W, HEAD_PAIR), lambda bi, hp, rc: (bi, rc, hp)),
        out_shape=jax.ShapeDtypeStruct((b, s, ATT_WIDTH), BF16),
        compiler_params=pltpu.CompilerParams(
            dimension_semantics=("parallel", "parallel", "arbitrary"),
            vmem_limit_bytes=V7X_VMEM_LIMIT),
        name="nbr_attention",
    )(q3, k3, v3, bias)


def _post_kernel(f_ref, a_ref, x_ref, wf_ref, wa_ref, g1_ref, b1_ref, wg_ref, wu_ref, wd_ref,
                 g2_ref, b2_ref, *rest, fuse_next):
    if fuse_next:
        wn_ref, o_ref, u_ref, q_ref, k_ref, v_ref, x1b_ref = rest
    else:
        o_ref, x1b_ref = rest
    subs = [slice(r, r + POST_SUB) for r in range(0, x_ref.shape[0], POST_SUB)]
    for rows in subs:
        mix = jnp.dot(f_ref[rows, :], wf_ref[...], preferred_element_type=F32)
        mix = mix + jnp.dot(a_ref[rows, :], wa_ref[...], preferred_element_type=F32)
        x1 = _layer_norm(ALPHA * x_ref[rows, :] + mix, g1_ref[...], b1_ref[...])
        o_ref[rows, :] = ALPHA * x1
        x1b_ref[rows, :] = x1.astype(BF16)
    outs = []
    for rows in subs:
        xb = x1b_ref[rows, :]
        for lo in range(0, D_FF, FF_CHUNK):
            sl = slice(lo, lo + FF_CHUNK)
            gate = jnp.dot(xb, wg_ref[:, sl], preferred_element_type=F32)
            up = jnp.dot(xb, wu_ref[:, sl], preferred_element_type=F32)
            hid = (gate * jax.nn.sigmoid(gate) * up).astype(BF16)
            o_ref[rows, :] += jnp.dot(hid, wd_ref[sl, :], preferred_element_type=F32)
        out = _layer_norm(o_ref[rows, :], g2_ref[...], b2_ref[...])
        o_ref[rows, :] = out
        outs.append(out.astype(BF16))
    if fuse_next:
        for rows, ob in zip(subs, outs):
            lo = 0
            for h_ref, scale in ((u_ref, None), (q_ref, ATT_SCALE), (k_ref, None), (v_ref, None)):
                width = h_ref.shape[-1]
                h = jnp.dot(ob, wn_ref[:, lo:lo + width], preferred_element_type=F32)
                if scale is not None:
                    h = h * scale
                h_ref[rows, :] = h.astype(BF16)
                lo += width


def _post(f2d, a2d, x2, p, w_b, layer, fuse_next):
    t = x2.shape[0]
    tm = TM_POST
    assert t % tm == 0 and tm % POST_SUB == 0 and D_FF % FF_CHUNK == 0, (t, tm)
    row = pl.BlockSpec((None, 1, D_MODEL), lambda i: (layer, 0, 0))
    resident = pl.Buffered(1)
    tile = lambda width: pl.BlockSpec((tm, width), lambda i: (i, 0))
    widths = (FOURIER_WIDTH, ATT_WIDTH, ATT_WIDTH, ATT_WIDTH)
    in_specs = [tile(FOURIER_WIDTH), tile(ATT_WIDTH), tile(D_MODEL),
                pl.BlockSpec((FOURIER_WIDTH, D_MODEL), lambda i: (0, 0), pipeline_mode=resident),
                pl.BlockSpec((ATT_WIDTH, D_MODEL), lambda i: (1, 0), pipeline_mode=resident),
                row, row,
                pl.BlockSpec((D_MODEL, D_FF), lambda i: (0, 0), pipeline_mode=resident),
                pl.BlockSpec((D_MODEL, D_FF), lambda i: (0, 0), pipeline_mode=resident),
                pl.BlockSpec((D_FF, D_MODEL), lambda i: (0, 0), pipeline_mode=resident),
                row, row]
    args = [f2d, a2d, x2, w_b["w_out"], w_b["w_out"], p["ln1_g"], p["ln1_b"],
            w_b["w_gate"], w_b["w_up"], w_b["w_down"], p["ln2_g"], p["ln2_b"]]
    out_specs = [tile(D_MODEL)]
    out_shape = [jax.ShapeDtypeStruct((t, D_MODEL), F32)]
    if fuse_next:
        in_specs.append(pl.BlockSpec((D_MODEL, IN_WIDTH), lambda i: (0, 0), pipeline_mode=resident))
        args.append(w_b["w_in_next"])
        out_specs += [tile(wd) for wd in widths]
        out_shape += [jax.ShapeDtypeStruct((t, wd), BF16) for wd in widths]
    outs = pl.pallas_call(
        functools.partial(_post_kernel, fuse_next=fuse_next),
        grid=(t // tm,),
        in_specs=in_specs,
        out_specs=out_specs,
        out_shape=out_shape,
        scratch_shapes=[pltpu.VMEM((tm, D_MODEL), BF16)],
        compiler_params=pltpu.CompilerParams(
            dimension_semantics=("parallel",), vmem_limit_bytes=V7X_VMEM_LIMIT),
        name="post_ffn",
    )(*args)
    return outs[0], (tuple(outs[1:]) if fuse_next else None)


CAST_IN_FOURIER = ("w_out", "w_gate", "w_up", "w_down")


def _layer(x3, p, w_b, layer, uqkv):
    b, s, d = x3.shape
    f1, f2, cs = _dft_constants(s)
    f1_b = jnp.asarray(f1).astype(BF16)
    f2_b = jnp.asarray(f2).astype(BF16)
    cs_b = jnp.asarray(cs).astype(BF16)

    x2d = x3.reshape(b * s, d)
    u, q, k, v = uqkv if uqkv is not None else _in_proj(x2d, p["w_in"], layer)
    fuse_next = layer + 1 < DEPTH
    names, to_cast = (), ()
    if w_b is None:
        names = CAST_IN_FOURIER + (("w_in_next",) if fuse_next else ())
        to_cast = tuple((p[n], layer) for n in CAST_IN_FOURIER)
        to_cast += ((p["w_in"], layer + 1),) if fuse_next else ()
    fmix, cast = _fourier_mix(u.reshape(b, s, FOURIER_WIDTH), f1_b, f2_b, cs_b, to_cast)
    if w_b is None:
        w_b = dict(zip(names, cast))
    att = _attention(q.reshape(b, s, ATT_WIDTH), k.reshape(b, s, ATT_WIDTH),
                     v.reshape(b, s, ATT_WIDTH), p["bias"], layer)
    x2, nxt = _post(fmix.reshape(b * s, FOURIER_WIDTH), att.reshape(b * s, ATT_WIDTH), x2d, p, w_b,
                    layer, fuse_next=fuse_next)
    return x2.reshape(b, s, d), w_b, nxt


def kernel(x_prompt, x_sample, w_in, w_out, rpb, ln1_g, ln1_b, w_gate, w_up, w_down, ln2_g, ln2_b):
    p = dict(
        w_in=w_in, w_out=w_out, w_gate=w_gate, w_up=w_up, w_down=w_down,
        ln1_g=ln1_g.reshape(DEPTH, 1, D_MODEL), ln1_b=ln1_b.reshape(DEPTH, 1, D_MODEL),
        ln2_g=ln2_g.reshape(DEPTH, 1, D_MODEL), ln2_b=ln2_b.reshape(DEPTH, 1, D_MODEL),
    )
    y_prompt, y_sample = x_prompt, x_sample
    uqkv_prompt = uqkv_sample = None
    p["bias"] = _bias_tables(rpb)
    for layer in range(DEPTH):
        y_prompt, w_b, uqkv_prompt = _layer(y_prompt, p, None, layer, uqkv_prompt)
        y_sample, _, uqkv_sample = _layer(y_sample, p, w_b, layer, uqkv_sample)
    return (y_prompt, y_sample)
```
